```python
import math
import jax
import jax.numpy as jnp
from jax import lax
import numpy as np

D_MODEL = 1024
BATCH = 4
SEQ = 8192
DEPTH = 2

GRID_W = 64
CTX_LEN = 256

RW_HEADS = 6
RW_HEAD_DIM = 64
RW_WIDTH = 384
RW_DECAY_LORA = 64
RW_AAA_LORA = 64
RW_GATE_LORA = 128
RW_GN_EPS = 64e-5

S5_GROUPS = 16
S5_GROUP_CH = 16
S5_WIDTH = 256
S5_STATE = 64
S5_DT_MIN = 1e-3
S5_DT_MAX = 1e-1

ML_HEADS = 4
ML_HEAD_DIM = 96
ML_WIDTH = 384
ML_CHUNK = 64
ML_EPS = 1e-5

D_FF = 2816
N_BRANCH = 3
N_MOD = 9
LN_EPS = 1e-5
DEEPNORM_ALPHA = (2.0 * DEPTH) ** 0.25
DEEPNORM_BETA = (8.0 * DEPTH) ** -0.25

CONV_COLS = 3 * RW_WIDTH + 2 * ML_WIDTH
IN_SPLITS = (CONV_COLS, ML_WIDTH, ML_WIDTH, 4 * ML_HEADS, S5_WIDTH, RW_DECAY_LORA, RW_AAA_LORA, RW_GATE_LORA, N_BRANCH * D_MODEL)
IN_WIDTH = CONV_COLS + 2 * ML_WIDTH + 4 * ML_HEADS + S5_WIDTH + RW_DECAY_LORA + RW_AAA_LORA + RW_GATE_LORA + N_BRANCH * D_MODEL

kernel_name = 'hybrid_rwkv7_s5_mlstm_prefix_dit'


def _split(z, sizes):
    parts, start = [], 0
    for s in sizes:
        parts.append(z[..., start:start + s])
        start += s
    return parts


def _standardize(x, eps):
    xf = x.astype(jnp.float32)
    xc = xf - jnp.mean(xf, axis=-1, keepdims=True)
    return xc * lax.rsqrt(jnp.mean(xc * xc, axis=-1, keepdims=True) + eps)


def _post_norm(x, delta, g, b):
    y = _standardize(DEEPNORM_ALPHA * x + delta, LN_EPS) * g.astype(jnp.float32) + b.astype(jnp.float32)
    return y.astype(x.dtype)


def _modulation(cvec, w, b):
    m = jnp.dot(jax.nn.silu(cvec), w) + b
    return jnp.split(m[..., None, :], N_MOD, axis=-1)


def _swiglu(u, wg, wu, wd):
    return jnp.dot(jax.nn.silu(jnp.dot(u, wg)) * jnp.dot(u, wu), wd)


def _ffn_half(x, shift, scale, gate, wg, wu, wd, g, b):
    u = x * (1.0 + scale) + shift
    return _post_norm(x, 0.5 * gate * _swiglu(u, wg, wu, wd), g, b)


def _conv_grid(z, w):
    b, l, ch = z.shape
    rows = l // GRID_W
    y = lax.conv_general_dilated(z.reshape(b, rows, GRID_W, ch), w[:, :, None, :].astype(z.dtype),
                                 window_strides=(1, 1), padding='SAME',
                                 dimension_numbers=('NHWC', 'HWIO', 'NHWC'), feature_group_count=ch)
    return y.reshape(b, l, ch)


def _conv_seq(z, w):
    wr = w[1].astype(z.dtype)
    zp = jnp.pad(z, ((0, 0), (1, 1), (0, 0)))
    return zp[:, :-2] * wr[0] + zp[:, 1:-1] * wr[1] + zp[:, 2:] * wr[2]


def _rwkv7_scan(r, w, k, v, kk, a, s0, reverse):
    def step(s, inp):
        r_t, w_t, k_t, v_t, kk_t, a_t = inp
        sa = jnp.einsum('bhvk,bhk->bhv', s, kk_t)
        s = s * w_t[:, :, None, :] - sa[..., None] * (kk_t * a_t)[:, :, None, :] + v_t[..., None] * k_t[:, :, None, :]
        return s, jnp.einsum('bhvk,bhk->bhv', s, r_t)
    xs = tuple(jnp.moveaxis(t, 1, 0) for t in (r, w, k, v, kk, a))
    s_fin, ys = lax.scan(step, s0, xs, reverse=reverse)
    return jnp.moveaxis(ys, 0, 1), s_fin


def _rwkv7_branch(r, k, v, w_dn, a_dn, g_dn, p, init):
    b, l, _ = r.shape
    f32 = jnp.float32
    hd = lambda t: t.astype(f32).reshape(b, l, RW_HEADS, RW_HEAD_DIM)
    rh, vh = hd(r), hd(v)
    kk = hd(k * p['rw_k_k'])
    kk = kk * lax.rsqrt(jnp.maximum(jnp.sum(kk * kk, axis=-1, keepdims=True), 1e-24))
    g = jnp.dot(jax.nn.sigmoid(g_dn), p['rw_g_up'])
    r_k = p['rw_r_k'].astype(f32)
    y_sum, bonus, finals = None, None, []
    for d, rev in enumerate((False, True)):
        w_log = -jax.nn.softplus(-(p['rw_w0'][d] + jnp.dot(jnp.tanh(w_dn), p['rw_w_up'][d]))) - 0.5
        decay = jnp.exp(-jnp.exp(w_log.astype(f32)))
        a = jax.nn.sigmoid(p['rw_a0'][d] + jnp.dot(a_dn, p['rw_a_up'][d]))
        k_d = hd(k * (1.0 + (a - 1.0) * p['rw_k_a']))
        s0 = jnp.zeros((b, RW_HEADS, RW_HEAD_DIM, RW_HEAD_DIM), f32) if init is None else init[d]
        y_d, s_d = _rwkv7_scan(rh, hd(decay), k_d, vh, kk, hd(a), s0, rev)
        bonus_d = jnp.sum(rh * k_d * r_k, axis=-1, keepdims=True) * vh
        y_sum = y_d if y_sum is None else y_sum + y_d
        bonus = bonus_d if bonus is None else bonus + bonus_d
        finals.append(s_d)
    y = _standardize(y_sum, RW_GN_EPS).reshape(b, l, RW_WIDTH) * p['rw_gn_g'].astype(f32) + p['rw_gn_b'].astype(f32)
    y = (y + bonus.reshape(b, l, RW_WIDTH)) * g
    return y.astype(r.dtype), finals


def _diag_combine(e1, e2):
    a1, b1 = e1
    a2, b2 = e2
    return a1 * a2, a2 * b1 + b2


def _s5_direction(ug, a_re, a_im, log_dt, b_re, b_im, c_re, c_im, x0, reverse):
    f32 = jnp.float32
    lam = lax.complex(jnp.minimum(a_re.astype(f32), -1e-4), a_im.astype(f32))
    a_bar = jnp.exp(lam * jnp.exp(log_dt.astype(f32))[:, None])
    b_bar = ((a_bar - 1.0) / lam)[..., None] * lax.complex(b_re.astype(f32), b_im.astype(f32))
    bu = jnp.einsum('blgh,gnh->blgn', ug.astype(jnp.complex64), b_bar)
    if reverse:
        bu = bu[:, ::-1]
    if x0 is not None:
        bu = bu.at[:, 0].add(a_bar * x0)
    a_seq = jnp.broadcast_to(a_bar, (1,) + bu.shape[1:])
    _, xs = lax.associative_scan(_diag_combine, (a_seq, bu), axis=1)
    y = jnp.einsum('blgn,ghn->blgh', xs, lax.complex(c_re.astype(f32), c_im.astype(f32))).real
    if reverse:
        y = y[:, ::-1]
    return y, xs[:, -1]


def _s5_branch(u, p, init):
    b, l, _ = u.shape
    f32 = jnp.float32
    ug = u.astype(f32).reshape(b, l, S5_GROUPS, S5_GROUP_CH)
    y = ug * p['s5_d'].astype(f32).reshape(S5_GROUPS, S5_GROUP_CH)
    finals = []
    for d, rev in enumerate((False, True)):
        y_d, x_d = _s5_direction(ug, p['s5_a_re'][d], p['s5_a_im'][d], p['s5_log_dt'][d],
                                 p['s5_b_re'][d], p['s5_b_im'][d], p['s5_c_re'][d], p['s5_c_im'][d],
                                 None if init is None else init[d], rev)
        y = y + y_d
        finals.append(x_d)
    y = jax.nn.gelu(y.reshape(b, l, S5_WIDTH))
    y = y * jax.nn.sigmoid(jnp.dot(y, p['s5_glu_w'].astype(f32)) + p['s5_glu_b'].astype(f32))
    return y.astype(u.dtype), finals


def _mlstm_direction(q, k, v, ig, fg, state0, reverse):
    if reverse:
        q, k, v, ig, fg = (t[:, ::-1] for t in (q, k, v, ig, fg))
    b, l, h, dh = q.shape
    nc = l // ML_CHUNK

    def chunks(t):
        t = t.reshape((b, nc, ML_CHUNK) + t.shape[2:])
        return jnp.moveaxis(jnp.moveaxis(t, 1, 0), 3, 2)

    lower_tri = jnp.tril(jnp.ones((ML_CHUNK, ML_CHUNK), dtype=bool))

    def step(carry, inp):
        c_prev, n_prev, m_prev = carry
        qc, kc, vc, ic, fc = inp
        bcum = jnp.cumsum(fc, axis=-1)
        logd = jnp.where(lower_tri, bcum[..., :, None] - bcum[..., None, :] + ic[..., None, :], -jnp.inf)
        log_inter = bcum + m_prev[..., None]
        m = jnp.maximum(log_inter, jnp.max(logd, axis=-1))
        inter = jnp.exp(log_inter - m)
        s = jnp.einsum('bhjd,bhsd->bhjs', qc, kc) * jnp.exp(logd - m[..., None])
        num = inter[..., None] * jnp.einsum('bhvk,bhjk->bhjv', c_prev, qc) + jnp.einsum('bhjs,bhsv->bhjv', s, vc)
        den = inter * jnp.einsum('bhk,bhjk->bhj', n_prev, qc) + jnp.sum(s, axis=-1)
        hc = num / jnp.maximum(jnp.abs(den), jnp.exp(-m))[..., None]
        b_last = bcum[..., -1]
        log_w = b_last[..., None] - bcum + ic
        m_new = jnp.maximum(b_last + m_prev, jnp.max(log_w, axis=-1))
        wgt = jnp.exp(log_w - m_new[..., None])
        dec = jnp.exp(b_last + m_prev - m_new)
        c_new = dec[..., None, None] * c_prev + jnp.einsum('bhs,bhsv,bhsk->bhvk', wgt, vc, kc)
        n_new = dec[..., None] * n_prev + jnp.einsum('bhs,bhsk->bhk', wgt, kc)
        return (c_new, n_new, m_new), hc

    state, hs = lax.scan(step, state0, tuple(chunks(t) for t in (q, k, v, ig, fg)))
    hs = jnp.moveaxis(jnp.moveaxis(hs, 2, 3), 0, 1).reshape(b, l, h, dh)
    if reverse:
        hs = hs[:, ::-1]
    return hs, state


def _mlstm_branch(q, k, v, o, gl, p, init):
    b, l, _ = q.shape
    f32 = jnp.float32
    hd = lambda t: t.astype(f32).reshape(b, l, ML_HEADS, ML_HEAD_DIM)
    qh = hd(jax.nn.silu(q))
    kh = hd(jax.nn.silu(k)) * (ML_HEAD_DIM ** -0.5)
    vh = hd(v)
    gates = gl.astype(f32).reshape(b, l, 2, 2, ML_HEADS) + p['ml_gate_b'].astype(f32)
    h_sum, finals = None, []
    for d, rev in enumerate((False, True)):
        if init is None:
            s0 = (jnp.zeros((b, ML_HEADS, ML_HEAD_DIM, ML_HEAD_DIM), f32),
                  jnp.zeros((b, ML_HEADS, ML_HEAD_DIM), f32),
                  jnp.zeros((b, ML_HEADS), f32))
        else:
            s0 = init[d]
        h_d, s_d = _mlstm_direction(qh, kh, vh, gates[:, :, d, 0], jax.nn.log_sigmoid(gates[:, :, d, 1]), s0, rev)
        h_sum = h_d if h_sum is None else h_sum + h_d
        finals.append(s_d)
    h_gated = jax.nn.sigmoid(hd(o)) * h_sum
    y = _standardize(h_gated, ML_EPS).reshape(b, l, ML_WIDTH) * p['ml_norm_g'].astype(f32)
    return y.astype(q.dtype), finals


def _mixer(u, p, init, grid):
    z = jnp.dot(u, p['w_in'])
    zc, ml_v, ml_o, ml_gl, s5_u, w_dn, a_dn, g_dn, br_gl = _split(z, IN_SPLITS)
    zc = _conv_grid(zc, p['conv_w']) if grid else _conv_seq(zc, p['conv_w'])
    rw_r, rw_k, rw_v, ml_q, ml_k = _split(zc, (RW_WIDTH, RW_WIDTH, RW_WIDTH, ML_WIDTH, ML_WIDTH))
    rw_y, rw_st = _rwkv7_branch(rw_r, rw_k, rw_v, w_dn, a_dn, g_dn, p, None if init is None else init[0])
    s5_y, s5_st = _s5_branch(s5_u, p, None if init is None else init[1])
    ml_y, ml_st = _mlstm_branch(ml_q, ml_k, ml_v, ml_o, ml_gl, p, None if init is None else init[2])
    return (rw_y, s5_y, ml_y, br_gl), (rw_st, s5_st, ml_st)


def _merge(branches, p):
    rw_y, s5_y, ml_y, br_gl = branches
    g_rw, g_s5, g_ml = jnp.split(jax.nn.sigmoid(br_gl + p['br_gate_b']), N_BRANCH, axis=-1)
    y = g_rw * jnp.dot(rw_y, p['up_rw']) + g_s5 * jnp.dot(s5_y, p['up_s5']) + g_ml * jnp.dot(ml_y, p['up_ml'])
    return jnp.dot(y, p['w_out'])


def setup_inputs(seed: int = 0) -> dict:
    key = jax.random.key(seed)
    ks = jax.random.split(key, 48)
    counter = [0]

    def nxt():
        counter[0] += 1
        return ks[counter[0] - 1]

    def nrm(shape, scale):
        return scale * jax.random.normal(nxt(), shape, jnp.float32)

    def uni(shape, lo, hi):
        return jax.random.uniform(nxt(), shape, jnp.float32, lo, hi)

    D = D_MODEL
    x = nrm((BATCH, SEQ, D), 1.0)
    c = nrm((BATCH, D), 1.0)
    ctx = nrm((BATCH, CTX_LEN, D), 1.0)
    c_ctx = nrm((D,), 1.0)
    ada_w = nrm((DEPTH, D, N_MOD * D), 0.5 * D ** -0.5)
    ada_b = nrm((DEPTH, N_MOD * D), 0.02)
    ln_g = 1.0 + nrm((DEPTH, 3, D), 0.02)
    ln_b = nrm((DEPTH, 3, D), 0.02)
    ffn_w_gate = nrm((DEPTH, 2, D, D_FF), D ** -0.5)
    ffn_w_up = nrm((DEPTH, 2, D, D_FF), D ** -0.5)
    ffn_w_down = nrm((DEPTH, 2, D_FF, D), DEEPNORM_BETA * D_FF ** -0.5)
    w_in = nrm((DEPTH, D, IN_WIDTH), D ** -0.5)
    conv_w = nrm((DEPTH, 3, 3, CONV_COLS), 0.15).at[:, 1, 1].add(0.6)
    rw_w0 = uni((DEPTH, 2, RW_WIDTH), -6.0, -1.0)
    rw_w_up = nrm((DEPTH, 2, RW_DECAY_LORA, RW_WIDTH), 0.5 * RW_DECAY_LORA ** -0.5)
    rw_a0 = nrm((DEPTH, 2, RW_WIDTH), 0.1)
    rw_a_up = nrm((DEPTH, 2, RW_AAA_LORA, RW_WIDTH), RW_AAA_LORA ** -0.5)
    rw_g_up = nrm((DEPTH, RW_GATE_LORA, RW_WIDTH), RW_GATE_LORA ** -0.5)
    rw_k_k = 0.85 + nrm((DEPTH, RW_WIDTH), 0.02)
    rw_k_a = 1.0 + nrm((DEPTH, RW_WIDTH), 0.02)
    rw_r_k = nrm((DEPTH, RW_HEADS, RW_HEAD_DIM), 0.1)
    rw_gn_g = 1.0 + nrm((DEPTH, RW_WIDTH), 0.02)
    rw_gn_b = nrm((DEPTH, RW_WIDTH), 0.02)
    s5_a_re = -0.5 + nrm((DEPTH, 2, S5_GROUPS, S5_STATE), 0.01)
    s5_a_im = math.pi * jnp.arange(S5_STATE, dtype=jnp.float32) + nrm((DEPTH, 2, S5_GROUPS, S5_STATE), 0.01)
    s5_log_dt = uni((DEPTH, 2, S5_GROUPS), math.log(S5_DT_MIN), math.log(S5_DT_MAX))
    s5_b_re = nrm((DEPTH, 2, S5_GROUPS, S5_STATE, S5_GROUP_CH), (2.0 * S5_GROUP_CH) ** -0.5)
    s5_b_im = nrm((DEPTH, 2, S5_GROUPS, S5_STATE, S5_GROUP_CH), (2.0 * S5_GROUP_CH) ** -0.5)
    s5_c_re = nrm((DEPTH, 2, S5_GROUPS, S5_GROUP_CH, S5_STATE), (0.5 * S5_STATE) ** -0.5)
    s5_c_im = nrm((DEPTH, 2, S5_GROUPS, S5_GROUP_CH, S5_STATE), (0.5 * S5_STATE) ** -0.5)
    s5_d = nrm((DEPTH, S5_WIDTH), 0.5)
    s5_glu_w = nrm((DEPTH, S5_WIDTH, S5_WIDTH), S5_WIDTH ** -0.5)
    s5_glu_b = nrm((DEPTH, S5_WIDTH), 0.02)
    ig_b = nrm((DEPTH, 2, 1, ML_HEADS), 0.1)
    fg_b = jnp.linspace(3.0, 6.0, ML_HEADS, dtype=jnp.float32) + nrm((DEPTH, 2, 1, ML_HEADS), 0.1)
    ml_gate_b = jnp.concatenate([ig_b, fg_b], axis=2)
    ml_norm_g = 1.0 + nrm((DEPTH, ML_WIDTH), 0.02)
    up_rw = nrm((DEPTH, RW_WIDTH, D), RW_WIDTH ** -0.5)
    up_s5 = nrm((DEPTH, S5_WIDTH, D), S5_WIDTH ** -0.5)
    up_ml = nrm((DEPTH, ML_WIDTH, D), ML_WIDTH ** -0.5)
    br_gate_b = nrm((DEPTH, N_BRANCH * D), 0.1)
    w_out = nrm((DEPTH, D, D), DEEPNORM_BETA * D ** -0.5)
    return {'x': x, 'c': c, 'ctx': ctx, 'c_ctx': c_ctx, 'ada_w': ada_w, 'ada_b': ada_b,
            'ln_g': ln_g, 'ln_b': ln_b, 'ffn_w_gate': ffn_w_gate, 'ffn_w_up': ffn_w_up, 'ffn_w_down': ffn_w_down,
            'w_in': w_in, 'conv_w': conv_w, 'rw_w0': rw_w0, 'rw_w_up': rw_w_up, 'rw_a0': rw_a0, 'rw_a_up': rw_a_up,
            'rw_g_up': rw_g_up, 'rw_k_k': rw_k_k, 'rw_k_a': rw_k_a, 'rw_r_k': rw_r_k, 'rw_gn_g': rw_gn_g,
            'rw_gn_b': rw_gn_b, 's5_a_re': s5_a_re, 's5_a_im': s5_a_im, 's5_log_dt': s5_log_dt,
            's5_b_re': s5_b_re, 's5_b_im': s5_b_im, 's5_c_re': s5_c_re, 's5_c_im': s5_c_im, 's5_d': s5_d,
            's5_glu_w': s5_glu_w, 's5_glu_b': s5_glu_b, 'ml_gate_b': ml_gate_b, 'ml_norm_g': ml_norm_g,
            'up_rw': up_rw, 'up_s5': up_s5, 'up_ml': up_ml, 'br_gate_b': br_gate_b, 'w_out': w_out}


def reference(x, c, ctx, c_ctx, ada_w, ada_b, ln_g, ln_b, ffn_w_gate, ffn_w_up, ffn_w_down, w_in, conv_w,
              rw_w0, rw_w_up, rw_a0, rw_a_up, rw_g_up, rw_k_k, rw_k_a, rw_r_k, rw_gn_g, rw_gn_b,
              s5_a_re, s5_a_im, s5_log_dt, s5_b_re, s5_b_im, s5_c_re, s5_c_im, s5_d, s5_glu_w, s5_glu_b,
              ml_gate_b, ml_norm_g, up_rw, up_s5, up_ml, br_gate_b, w_out):
    h = ctx
    for i in range(DEPTH):
        last = i == DEPTH - 1
        p = dict(w_in=w_in[i], conv_w=conv_w[i], rw_w0=rw_w0[i], rw_w_up=rw_w_up[i], rw_a0=rw_a0[i],
                 rw_a_up=rw_a_up[i], rw_g_up=rw_g_up[i], rw_k_k=rw_k_k[i], rw_k_a=rw_k_a[i], rw_r_k=rw_r_k[i],
                 rw_gn_g=rw_gn_g[i], rw_gn_b=rw_gn_b[i], s5_a_re=s5_a_re[i], s5_a_im=s5_a_im[i],
                 s5_log_dt=s5_log_dt[i], s5_b_re=s5_b_re[i], s5_b_im=s5_b_im[i], s5_c_re=s5_c_re[i],
                 s5_c_im=s5_c_im[i], s5_d=s5_d[i], s5_glu_w=s5_glu_w[i], s5_glu_b=s5_glu_b[i],
                 ml_gate_b=ml_gate_b[i], ml_norm_g=ml_norm_g[i], up_rw=up_rw[i], up_s5=up_s5[i],
                 up_ml=up_ml[i], br_gate_b=br_gate_b[i], w_out=w_out[i])
        mx = _modulation(c, ada_w[i], ada_b[i])
        mc = _modulation(c_ctx, ada_w[i], ada_b[i])
        x = _ffn_half(x, mx[0], mx[1], mx[2], ffn_w_gate[i, 0], ffn_w_up[i, 0], ffn_w_down[i, 0], ln_g[i, 0], ln_b[i, 0])
        h = _ffn_half(h, mc[0], mc[1], mc[2], ffn_w_gate[i, 0], ffn_w_up[i, 0], ffn_w_down[i, 0], ln_g[i, 0], ln_b[i, 0])
        uc = h * (1.0 + mc[4]) + mc[3]
        ctx_branches, ctx_states = _mixer(uc, p, None, False)
        ux = x * (1.0 + mx[4]) + mx[3]
        lat_branches, _ = _mixer(ux, p, ctx_states, True)
        x = _post_norm(x, mx[5] * _merge(lat_branches, p), ln_g[i, 1], ln_b[i, 1])
        x = _ffn_half(x, mx[6], mx[7], mx[8], ffn_w_gate[i, 1], ffn_w_up[i, 1], ffn_w_down[i, 1], ln_g[i, 2], ln_b[i, 2])
        if not last:
            h = _post_norm(h, mc[5] * _merge(ctx_branches, p), ln_g[i, 1], ln_b[i, 1])
            h = _ffn_half(h, mc[6], mc[7], mc[8], ffn_w_gate[i, 1], ffn_w_up[i, 1], ffn_w_down[i, 1], ln_g[i, 2], ln_b[i, 2])
    return x
```

```python
import functools
import math

import numpy as np
import jax
import jax.numpy as jnp
from jax import lax
from jax.experimental import pallas as pl
from jax.experimental.pallas import tpu as pltpu

F32 = jnp.float32
BF16 = jnp.bfloat16

D_MODEL = 1024
DEPTH = 2
GRID_W = 64

RW_HEADS = 6
RW_HEAD_DIM = 64
RW_WIDTH = 384
RW_DECAY_LORA = 64
RW_AAA_LORA = 64
RW_GATE_LORA = 128
RW_GN_EPS = 64e-5

S5_GROUPS = 16
S5_GROUP_CH = 16
S5_WIDTH = 256
S5_STATE = 64

ML_HEADS = 4
ML_HEAD_DIM = 96
ML_WIDTH = 384
ML_EPS = 1e-5

D_FF = 2816
N_BRANCH = 3
N_MOD = 9
LN_EPS = 1e-5
DEEPNORM_ALPHA = (2.0 * DEPTH) ** 0.25

CONV_COLS = 3 * RW_WIDTH + 2 * ML_WIDTH

LANES = 128
SUBLANES = 8
CHUNK = 64
S5_CHUNK = 8
S5_HALF = 128
S5_HSTATE = 512

Z_BR = 0
Z_CONV = 3072
Z_MLV = Z_CONV + CONV_COLS
Z_MLO = Z_MLV + ML_WIDTH
Z_LORA = Z_MLO + ML_WIDTH
Z_S5 = Z_LORA + 128
Z_GDN = Z_S5 + S5_WIDTH
Z_MLG = Z_GDN + RW_GATE_LORA
Z_WIDTH = Z_MLG + 128

VMEM_LIMIT = 56 * 1024 * 1024

_NN = (((1,), (0,)), ((), ()))
_NT = (((1,), (1,)), ((), ()))
_TN = (((0,), (0,)), ((), ()))


def _dot(a, b, dims=_NN):
    return lax.dot_general(a, b, dims, preferred_element_type=F32)


def _mm(a, b, dims=_NN):
    return _dot(a.astype(BF16), b.astype(BF16), dims)


def _split(a, n):
    parts, r = [], a
    for _ in range(n):
        p = r.astype(BF16)
        parts.append(p)
        r = r - p.astype(F32)
    return parts


def _mm_xr(a, b_exact, dims=_NN):
    ps = _split(a, 3)
    return _dot(ps[0], b_exact, dims) + (_dot(ps[1], b_exact, dims) + _dot(ps[2], b_exact, dims))


def _mm_xl(a_exact, b, dims=_NN):
    ps = _split(b, 3)
    return _dot(a_exact, ps[0], dims) + (_dot(a_exact, ps[1], dims) + _dot(a_exact, ps[2], dims))


def _mm3(a, b, dims=_NN):
    ah, al = _split(a, 2)
    bh, bl = _split(b, 2)
    return _dot(ah, bh, dims) + (_dot(al, bh, dims) + _dot(ah, bl, dims))


def _softplus(x):
    return jnp.maximum(x, 0.0) + jnp.log1p(jnp.exp(-jnp.abs(x)))


def _layer_norm(y, g, b):
    yc = y - jnp.mean(y, axis=-1, keepdims=True)
    return yc * lax.rsqrt(jnp.mean(yc * yc, axis=-1, keepdims=True) + LN_EPS) * g + b


def _pick_tile(cands, *lens):
    for t in cands:
        if all(n % t == 0 for n in lens):
            return t
    raise ValueError(f"no tile in {cands} divides {lens}")


def _params(*sem):
    return pltpu.CompilerParams(dimension_semantics=sem, vmem_limit_bytes=VMEM_LIMIT)


def _mod_row(row0, nlat, seq, nb):
    return jnp.where(row0 < nlat, row0 // seq, nb)


def _mod_kernel(cv_ref, w_ref, b_ref, o_ref):
    s = cv_ref[...]
    s = s * jax.nn.sigmoid(s)
    o_ref[0] = _mm3(s, w_ref[0]) + b_ref[0]


def _modulation(cv, ada_w, ada_b):
    depth, d, nd = ada_w.shape
    rows = cv.shape[0]
    return pl.pallas_call(
        _mod_kernel,
        grid=(depth, nd // d),
        in_specs=[pl.BlockSpec((rows, d), lambda l, j: (0, 0)),
                  pl.BlockSpec((1, d, d), lambda l, j: (l, 0, j)),
                  pl.BlockSpec((1, 1, d), lambda l, j: (l, 0, j))],
        out_specs=pl.BlockSpec((1, rows, d), lambda l, j: (l, 0, j)),
        out_shape=jax.ShapeDtypeStruct((depth, rows, nd), F32),
        compiler_params=_params("parallel", "parallel"),
        name="modulation",
    )(cv, ada_w, ada_b.reshape(depth, 1, nd))


def _ffn_kernel(x_ref, m_ref, wg_ref, wu_ref, wd_ref, g_ref, b_ref, o_ref, u_scr, acc_scr,
                *, k0, nlat, seq, nb, tm, nf):
    d = D_MODEL
    f = pl.program_id(1)
    mrow = _mod_row(pl.program_id(0) * tm, nlat, seq, nb)

    @pl.when(f == 0)
    def _():
        shift = m_ref[pl.ds(mrow, 1), k0 * d:(k0 + 1) * d]
        scale = m_ref[pl.ds(mrow, 1), (k0 + 1) * d:(k0 + 2) * d]
        u_scr[...] = (x_ref[...] * (1.0 + scale) + shift).astype(BF16)
        acc_scr[...] = jnp.zeros_like(acc_scr)

    u = u_scr[...]
    g = _dot(u, wg_ref[...])
    up = _dot(u, wu_ref[...])
    h = (g * jax.nn.sigmoid(g) * up).astype(BF16)
    acc_scr[...] += _dot(h, wd_ref[...])

    @pl.when(f == nf - 1)
    def _():
        gate = m_ref[pl.ds(mrow, 1), (k0 + 2) * d:(k0 + 3) * d]
        y = DEEPNORM_ALPHA * x_ref[...] + (0.5 * gate) * acc_scr[...]
        o_ref[...] = _layer_norm(y, g_ref[...], b_ref[...])


def _ffn_half(xs, mods, k0, wg, wu, wd, g, b, dims):
    nb, seq, ctx = dims
    r, d = xs.shape
    nlat = nb * seq
    tm = _pick_tile((1024, 512, 256, 128, 64), nlat, nb * ctx)
    tf = 256
    nf = D_FF // tf
    kern = functools.partial(_ffn_kernel, k0=k0, nlat=nlat, seq=seq, nb=nb, tm=tm, nf=nf)
    return pl.pallas_call(
        kern,
        grid=(r // tm, nf),
        in_specs=[pl.BlockSpec((tm, d), lambda i, f: (i, 0)),
                  pl.BlockSpec(mods.shape, lambda i, f: (0, 0)),
                  pl.BlockSpec((d, tf), lambda i, f: (0, f)),
                  pl.BlockSpec((d, tf), lambda i, f: (0, f)),
                  pl.BlockSpec((tf, d), lambda i, f: (f, 0)),
                  pl.BlockSpec((1, d), lambda i, f: (0, 0)),
                  pl.BlockSpec((1, d), lambda i, f: (0, 0))],
        out_specs=pl.BlockSpec((tm, d), lambda i, f: (i, 0)),
        out_shape=jax.ShapeDtypeStruct((r, d), F32),
        scratch_shapes=[pltpu.VMEM((tm, d), BF16), pltpu.VMEM((tm, d), F32)],
        compiler_params=_params("parallel", "arbitrary"),
        name="ffn_half",
    )(xs, mods, wg, wu, wd, g.reshape(1, d), b.reshape(1, d))


def _inproj_kernel(x_ref, m_ref, w_ref, o_ref, u_scr, *, nlat, seq, nb, tm):
    d = D_MODEL
    mrow = _mod_row(pl.program_id(0) * tm, nlat, seq, nb)

    @pl.when(pl.program_id(1) == 0)
    def _():
        shift = m_ref[pl.ds(mrow, 1), 3 * d:4 * d]
        scale = m_ref[pl.ds(mrow, 1), 4 * d:5 * d]
        u_scr[...] = (x_ref[...] * (1.0 + scale) + shift).astype(BF16)

    o_ref[...] = _dot(u_scr[...], w_ref[...])


def _in_proj(xs, mods, w_in_p, dims):
    nb, seq, ctx = dims
    r, d = xs.shape
    nlat = nb * seq
    tm = _pick_tile((1024, 512, 256, 128, 64), nlat, nb * ctx)
    tn = 640
    kern = functools.partial(_inproj_kernel, nlat=nlat, seq=seq, nb=nb, tm=tm)
    return pl.pallas_call(
        kern,
        grid=(r // tm, Z_WIDTH // tn),
        in_specs=[pl.BlockSpec((tm, d), lambda i, j: (i, 0)),
                  pl.BlockSpec(mods.shape, lambda i, j: (0, 0)),
                  pl.BlockSpec((d, tn), lambda i, j: (0, j))],
        out_specs=pl.BlockSpec((tm, tn), lambda i, j: (i, j)),
        out_shape=jax.ShapeDtypeStruct((r, Z_WIDTH), F32),
        scratch_shapes=[pltpu.VMEM((tm, d), BF16)],
        compiler_params=_params("parallel", "arbitrary"),
        name="in_proj",
    )(xs, mods, w_in_p)


def _conv_kernel(prev_ref, cur_ref, next_ref, w_ref, o_ref, *, tmc, nlat, seq, ctx):
    row0 = pl.program_id(0) * tmc
    is_lat = row0 < nlat
    off = jnp.where(is_lat, lax.rem(row0, seq), lax.rem(jnp.maximum(row0 - nlat, 0), ctx))
    seglen = jnp.where(is_lat, seq, ctx)
    has_prev = jnp.logical_and(is_lat, off > 0)
    has_next = jnp.logical_and(is_lat, off + tmc < seglen)
    per = jnp.where(is_lat, GRID_W, ctx)
    pos = lax.broadcasted_iota(jnp.int32, (tmc, 1), 0) + off
    col = lax.rem(pos, per)
    m_left = (col != 0).astype(F32)
    m_right = (col != per - 1).astype(F32)
    cur = cur_ref[...]
    prev = jnp.where(has_prev, prev_ref[...], 0.0)
    nxt = jnp.where(has_next, next_ref[...], 0.0)
    if tmc > GRID_W:
        up = jnp.concatenate([prev, cur[:tmc - GRID_W]], axis=0)
        dn = jnp.concatenate([cur[GRID_W:], nxt], axis=0)
    else:
        up, dn = prev, nxt
    lat_f = jnp.where(is_lat, 1.0, 0.0)
    w = w_ref[...]

    def taps(blk, dr):
        left = pltpu.roll(blk, 1, 0) * m_left
        right = pltpu.roll(blk, tmc - 1, 0) * m_right
        return left * w[dr, 0:1, :] + blk * w[dr, 1:2, :] + right * w[dr, 2:3, :]

    o_ref[...] = taps(cur, 1) + lat_f * (taps(up, 0) + taps(dn, 2))


def _short_conv(z, conv_w, dims):
    nb, seq, ctx = dims
    r = z.shape[0]
    nlat = nb * seq
    tmc = _pick_tile((256, 128, 64), seq, ctx)
    cw = RW_WIDTH
    hb = tmc // GRID_W
    nblk = r // GRID_W
    c0 = Z_CONV // cw
    kern = functools.partial(_conv_kernel, tmc=tmc, nlat=nlat, seq=seq, ctx=ctx)
    return pl.pallas_call(
        kern,
        grid=(r // tmc, CONV_COLS // cw),
        in_specs=[pl.BlockSpec((GRID_W, cw), lambda i, c: (jnp.maximum(i * hb - 1, 0), c0 + c)),
                  pl.BlockSpec((tmc, cw), lambda i, c: (i, c0 + c)),
                  pl.BlockSpec((GRID_W, cw), lambda i, c: (jnp.minimum(i * hb + hb, nblk - 1), c0 + c)),
                  pl.BlockSpec((3, 3, cw), lambda i, c: (0, 0, c))],
        out_specs=pl.BlockSpec((tmc, cw), lambda i, c: (i, c)),
        out_shape=jax.ShapeDtypeStruct((r, CONV_COLS), F32),
        compiler_params=_params("parallel", "parallel"),
        name="short_conv",
    )(z, z, z, conv_w)


def _chunk_block(b, s, rev, nb, nc, nl):
    cc = (nc - 1 - s) if rev else s
    lc = (nl - 1 - (s - nc)) if rev else (s - nc)
    return jnp.where(s < nc, nb * nl + b * nc + cc, b * nl + lc)


def _pair_masks():
    lane = lax.broadcasted_iota(jnp.int32, (1, LANES), 1)
    m0 = (lane < RW_HEAD_DIM).astype(F32)
    return m0, 1.0 - m0


def _bd(y, m0, m1):
    n = y.shape[1] // LANES
    if n > 1:
        m0 = jnp.concatenate([m0] * n, axis=1)
        m1 = jnp.concatenate([m1] * n, axis=1)
    return jnp.concatenate([y * m0, y * m1], axis=0)


def _rwkv_chunk_kernel(r_ref, k_ref, v_ref, lo_ref, kk_ref, ka_ref, rk_ref, w0_ref, wup_ref, a0_ref, aup_ref,
                       bones_ref, y1_ref, rq_ref, pt_ref, qt_ref, bonus_ref):
    c = CHUNK
    r = r_ref[...]
    k = k_ref[...]
    v = v_ref[...]
    lo = lo_ref[...]
    bones = bones_ref[...]
    kk = k * kk_ref[...]
    kk = kk * lax.rsqrt(jnp.maximum(_mm_xr(kk * kk, bones), 1e-24))
    tw = jnp.tanh(lo)
    m0, m1 = _pair_masks()

    ti = lax.broadcasted_iota(jnp.int32, (c, c), 0)
    tj = lax.broadcasted_iota(jnp.int32, (c, c), 1)
    pt_i = lax.broadcasted_iota(jnp.int32, (c, LANES), 0)
    pt_j = lax.rem(lax.broadcasted_iota(jnp.int32, (c, LANES), 1), c)
    eye_p = (pt_i == pt_j).astype(F32)

    bonus = jnp.zeros_like(r)
    for d in range(2):
        rev = d == 1
        tri = ((tj >= ti) if rev else (tj <= ti)).astype(BF16)
        strict = ((pt_j > pt_i) if rev else (pt_j < pt_i)).astype(F32)
        incl = strict + eye_p
        wl = w0_ref[d] + _mm(tw, wup_ref[d])
        lw = -jnp.exp(-_softplus(-wl) - 0.5)
        a = jax.nn.sigmoid(a0_ref[d] + _mm(lo, aup_ref[d]))
        kd = k * (1.0 + (a - 1.0) * ka_ref[...])
        bb = kk * a
        bonus = bonus + _mm_xr(r * kd * rk_ref[...], bones) * v
        cs = _mm_xl(tri, lw)
        csp = cs - lw
        tot = cs[0:1, :] if rev else cs[c - 1:c, :]
        mid = 0.5 * tot
        e_dn = jnp.exp(mid - cs)
        a_hat = kk * jnp.exp(csp - mid)
        r_hat = r * jnp.exp(cs - mid)
        b_hat = bb * e_dn
        k_hat = kd * e_dn
        a_st = kk * jnp.exp(csp)
        r_st = r * jnp.exp(cs)
        e_end = jnp.exp(tot - cs)
        b_end = bb * e_end
        k_end = kd * e_end
        w_c = jnp.exp(tot)
        for p in range(RW_HEADS // 2):
            sl = slice(p * LANES, (p + 1) * LANES)
            vp = v[:, sl]
            lhs = jnp.concatenate([a_hat[:, sl], r_hat[:, sl]], axis=0)
            rhs_t = jnp.concatenate([_bd(b_hat[:, sl], m0, m1), _bd(k_hat[:, sl], m0, m1)], axis=0)
            g4 = _mm3(lhs, rhs_t, _NT)
            l_ab = g4[0:c, 0:LANES] * strict
            l_ak = g4[0:c, LANES:] * strict
            m_rb = g4[c:, 0:LANES] * incl
            m_rk = g4[c:, LANES:] * incl
            t_inv = eye_p - l_ab
            lp = l_ab
            for _ in range(5):
                lp = _mm3(lp, _bd(lp, m0, m1))
                t_inv = t_inv + _mm3(t_inv, _bd(lp, m0, m1))
            x0 = _mm(l_ak, _bd(vp, m0, m1))
            ua = _mm3(t_inv, _bd(jnp.concatenate([x0, a_st[:, sl]], axis=1), m0, m1))
            mu = _mm(m_rb, _bd(ua, m0, m1))
            y1_ref[d, :, sl] = _mm(m_rk, _bd(vp, m0, m1)) - mu[:, 0:LANES]
            rq_ref[d, :, sl] = r_st[:, sl] - mu[:, LANES:]
            u1 = ua[:, 0:LANES]
            a_m = ua[:, LANES:]
            gp = _mm(b_end[:, sl], a_m, _TN)
            pt_ref[d, :, sl] = eye_p * w_c[:, sl] - (gp[0:c] * m0 + gp[c:] * m1)
            hp = _mm(jnp.concatenate([k_end[:, sl], b_end[:, sl]], axis=0),
                     jnp.concatenate([vp, -u1], axis=0), _TN)
            qt_ref[d, :, sl] = hp[0:c] * m0 + hp[c:] * m1
    bonus_ref[...] = bonus


def _rwkv_chunk(zc, z, p, dims):
    r = zc.shape[0]
    w = RW_WIDTH
    c = CHUNK
    full = lambda shape: pl.BlockSpec(shape, lambda i: (0,) * len(shape))
    out4 = jax.ShapeDtypeStruct((2, r, w), F32)
    return pl.pallas_call(
        _rwkv_chunk_kernel,
        grid=(r // c,),
        in_specs=[pl.BlockSpec((c, w), lambda i: (i, 0)),
                  pl.BlockSpec((c, w), lambda i: (i, 1)),
                  pl.BlockSpec((c, w), lambda i: (i, 2)),
                  pl.BlockSpec((c, LANES), lambda i: (i, Z_LORA // LANES)),
                  full((1, w)), full((1, w)), full((1, w)),
                  full((2, 1, w)), full((2, LANES, w)), full((2, 1, w)), full((2, LANES, w)),
                  full((w, w))],
        out_specs=[pl.BlockSpec((2, c, w), lambda i: (0, i, 0))] * 4 + [pl.BlockSpec((c, w), lambda i: (i, 0))],
        out_shape=[out4, out4, out4, out4, jax.ShapeDtypeStruct((r, w), F32)],
        compiler_params=_params("parallel"),
        name="rwkv_chunk",
    )(zc, zc, zc, z, p['k_k'], p['k_a'], p['r_k'], p['w0'], p['w_up'], p['a0'], p['a_up'], p['bones64'])


def _rwkv_scan_kernel(y1f, rqf, ptf, qtf, y1r, rqr, ptr, qtr, yf_ref, yr_ref, st_scr):
    c = CHUNK

    @pl.when(pl.program_id(1) == 0)
    def _():
        st_scr[...] = jnp.zeros_like(st_scr)

    m0, m1 = _pair_masks()
    for d, (y1, rq, pt, qt, yo) in enumerate(((y1f, rqf, ptf, qtf, yf_ref), (y1r, rqr, ptr, qtr, yr_ref))):
        for p in range(RW_HEADS // 2):
            sl = slice(p * LANES, (p + 1) * LANES)
            st = st_scr[d, :, sl]
            lhs = jnp.concatenate([rq[0, :, sl], pt[0, :, sl]], axis=0)
            res = _mm3(lhs, _bd(st, m0, m1))
            yo[:, sl] = y1[0, :, sl] + res[0:c]
            st_scr[d, :, sl] = res[c:] + qt[0, :, sl]


def _rwkv_scan(y1, rq, pt, qt, dims):
    nb, seq, ctx = dims
    r = y1.shape[1]
    w = RW_WIDTH
    c = CHUNK
    nc, nl = ctx // c, seq // c

    def spec(d):
        return pl.BlockSpec((1, c, w), lambda b, s: (d, _chunk_block(b, s, d == 1, nb, nc, nl), 0))

    def ospec(d):
        return pl.BlockSpec((c, w), lambda b, s: (_chunk_block(b, s, d == 1, nb, nc, nl), 0))

    out = jax.ShapeDtypeStruct((r, w), F32)
    return pl.pallas_call(
        _rwkv_scan_kernel,
        grid=(nb, nc + nl),
        in_specs=[spec(0)] * 4 + [spec(1)] * 4,
        out_specs=[ospec(0), ospec(1)],
        out_shape=[out, out],
        scratch_shapes=[pltpu.VMEM((2, c, w), F32)],
        compiler_params=_params("parallel", "arbitrary"),
        name="rwkv_scan",
    )(y1, rq, pt, qt, y1, rq, pt, qt)


def _mlstm_kernel(qf, kf, vf, gcf, gtf, qr, kr, vr, gcr, gtr, gbr_ref, gbc_ref, bmask_ref,
                  hf_ref, hr_ref, c_scr, n_scr, m_scr):
    c = CHUNK
    nh, dh = ML_HEADS, ML_HEAD_DIM

    @pl.when(pl.program_id(1) == 0)
    def _():
        c_scr[...] = jnp.zeros_like(c_scr)
        n_scr[...] = jnp.zeros_like(n_scr)
        m_scr[...] = jnp.zeros_like(m_scr)

    lane = lax.broadcasted_iota(jnp.int32, (1, ML_WIDTH), 1)
    hmask = [jnp.logical_and(lane >= h * dh, lane < (h + 1) * dh).astype(F32) for h in range(nh)]
    lane1 = lax.broadcasted_iota(jnp.int32, (1, LANES), 1)
    ti = lax.broadcasted_iota(jnp.int32, (c, c), 0)
    tj = lax.broadcasted_iota(jnp.int32, (c, c), 1)
    bmask = bmask_ref[...]

    for d, (q_ref, k_ref, v_ref, gc_ref, gt_ref, h_ref) in enumerate(
            ((qf, kf, vf, gcf, gtf, hf_ref), (qr, kr, vr, gcr, gtr, hr_ref))):
        rev = d == 1
        valid = (tj >= ti) if rev else (tj <= ti)
        tri = valid.astype(BF16)
        q = q_ref[...]
        q = q * jax.nn.sigmoid(q)
        k = k_ref[...]
        k = k * jax.nn.sigmoid(k) * (dh ** -0.5)
        v = v_ref[...]
        gc = gc_ref[...] + gbr_ref[...]
        gt = gt_ref[0] + gbc_ref[...]
        fc = -_softplus(-gc)
        ft = -_softplus(-gt)
        bc_cols = _mm_xl(tri, fc)
        bc_rows = _mm_xr(ft, tri, _NT)
        c_prev = c_scr[d]
        n_prev = n_scr[d]
        m_all = m_scr[d]
        qc = _mm(q, c_prev)
        s_parts, inter_e, den_parts, wgt_e, dec_e, m_new_all = [], 0.0, [], 0.0, 0.0, 0.0
        for h in range(nh):
            ci, cf = d * 8 + h, d * 8 + 4 + h
            bcc = bc_cols[:, cf:cf + 1]
            icc = gc[:, ci:ci + 1]
            bcr = bc_rows[cf:cf + 1, :]
            icr = gt[ci:ci + 1, :]
            b_last = bcc[0:1, :] if rev else bcc[c - 1:c, :]
            m_prev = m_all[:, h:h + 1]
            logd = jnp.where(valid, bcc - bcr + icr, -jnp.inf)
            log_inter = bcc + m_prev
            m = jnp.maximum(log_inter, jnp.max(logd, axis=-1, keepdims=True))
            inter = jnp.exp(log_inter - m)
            sm = _mm(q * hmask[h], k, _NT) * jnp.exp(logd - m)
            s_parts.append(sm)
            nq = jnp.sum(q * n_prev * hmask[h], axis=-1, keepdims=True)
            den = inter * nq + jnp.sum(sm, axis=-1, keepdims=True)
            den_parts.append(jnp.maximum(jnp.abs(den), jnp.exp(-m)) * hmask[h])
            inter_e = inter_e + inter * hmask[h]
            log_w = b_last - bcc + icc
            m_new = jnp.maximum(b_last + m_prev, jnp.max(log_w, axis=0, keepdims=True))
            wgt_e = wgt_e + jnp.exp(log_w - m_new) * hmask[h]
            dec_e = dec_e + jnp.exp(b_last + m_prev - m_new) * hmask[h]
            m_new_all = m_new_all + m_new * (lane1 == h).astype(F32)
        s_all = jnp.concatenate(s_parts, axis=1)
        v_bd = jnp.concatenate([v * hmask[h] for h in range(nh)], axis=0)
        num = inter_e * qc + _mm(s_all, v_bd)
        den_e = den_parts[0] + den_parts[1] + den_parts[2] + den_parts[3]
        h_ref[...] = num / den_e
        kw = k * wgt_e
        c_scr[d] = dec_e * c_prev + bmask * _mm(kw, v, _TN)
        n_scr[d] = dec_e * n_prev + jnp.sum(kw, axis=0, keepdims=True)
        m_scr[d] = m_new_all


def _mlstm(zc, z, glt, p, dims):
    nb, seq, ctx = dims
    r = zc.shape[0]
    w = ML_WIDTH
    c = CHUNK
    nc, nl = ctx // c, seq // c

    def blk(d):
        return lambda b, s: _chunk_block(b, s, d == 1, nb, nc, nl)

    def specs(d):
        f = blk(d)
        return [pl.BlockSpec((c, w), lambda b, s: (f(b, s), 3)),
                pl.BlockSpec((c, w), lambda b, s: (f(b, s), 4)),
                pl.BlockSpec((c, w), lambda b, s: (f(b, s), Z_MLV // w)),
                pl.BlockSpec((c, LANES), lambda b, s: (f(b, s), Z_MLG // LANES)),
                pl.BlockSpec((1, 16, c), lambda b, s: (f(b, s), 0, 0))]

    full = lambda shape: pl.BlockSpec(shape, lambda b, s: (0,) * len(shape))
    out = jax.ShapeDtypeStruct((r, w), F32)
    return pl.pallas_call(
        _mlstm_kernel,
        grid=(nb, nc + nl),
        in_specs=specs(0) + specs(1) + [full((1, LANES)), full((16, 1)), full((w, w))],
        out_specs=[pl.BlockSpec((c, w), lambda b, s: (blk(0)(b, s), 0)),
                   pl.BlockSpec((c, w), lambda b, s: (blk(1)(b, s), 0))],
        out_shape=[out, out],
        scratch_shapes=[pltpu.VMEM((2, w, w), F32), pltpu.VMEM((2, 1, w), F32), pltpu.VMEM((2, 1, LANES), F32)],
        compiler_params=_params("parallel", "arbitrary"),
        name="mlstm",
    )(zc, zc, z, z, glt, zc, zc, z, z, glt, p['gb_row'], p['gb_col'], p['bmask96'])


def _s5_kernel(u_ref, kt_ref, wb_ref, wc_ref, a_ref, x0_ref, yin_ref, y_ref, xf_ref, lhs_scr, bc_scr, xp_scr,
               *, n):
    del yin_ref
    cs = S5_CHUNK
    hs = S5_HSTATE
    for i in range(cs):
        lhs_scr[:, i * S5_HALF:(i + 1) * S5_HALF] = u_ref[pl.ds(i, n, stride=cs), :].astype(BF16)
    lhs = lhs_scr[...]
    yacc = _dot(lhs, kt_ref[0])
    for d in range(2):
        bc_scr[...] = _dot(lhs, wb_ref[d, 0])
        a = a_ref[d, 0]
        ar, ai = a[:, :hs], a[:, hs:]
        x0 = x0_ref[d, 0, 0]

        def body(t, x, d=d, ar=ar, ai=ai):
            idx = (n - 1 - t) if d == 1 else t
            xr, xi = x
            xp_scr[pl.ds(idx, 1), :] = jnp.concatenate([xr, xi], axis=1)
            row = bc_scr[pl.ds(idx, 1), :]
            return (ar * xr - ai * xi + row[:, :hs], ar * xi + ai * xr + row[:, hs:])

        xr, xi = lax.fori_loop(0, n, body, (x0[:, :hs], x0[:, hs:]))
        xf_ref[d, 0, 0] = jnp.concatenate([xr, xi], axis=1)
        yacc = yacc + _dot(xp_scr[...].astype(BF16), wc_ref[d, 0])
    for j in range(cs):
        y_ref[pl.ds(j, n, stride=cs), :] = yacc[:, j * S5_HALF:(j + 1) * S5_HALF]


def _s5_segment(z, y_prev, x0, sp, seglen, seg_base, nb):
    r = z.shape[0]
    n = seglen // S5_CHUNK
    kw = S5_CHUNK * S5_HALF
    hs2 = 2 * S5_HSTATE
    kern = functools.partial(_s5_kernel, n=n)
    in_specs = [pl.BlockSpec((seglen, S5_HALF), lambda b, g: (seg_base + b, Z_S5 // S5_HALF + g)),
                pl.BlockSpec((1, kw, kw), lambda b, g: (g, 0, 0)),
                pl.BlockSpec((2, 1, kw, hs2), lambda b, g: (0, g, 0, 0)),
                pl.BlockSpec((2, 1, hs2, kw), lambda b, g: (0, g, 0, 0)),
                pl.BlockSpec((2, 1, 1, hs2), lambda b, g: (0, g, 0, 0)),
                pl.BlockSpec((2, 1, 1, 1, hs2), lambda b, g: (0, b, g, 0, 0)),
                pl.BlockSpec(memory_space=pl.ANY)]
    return pl.pallas_call(
        kern,
        grid=(nb, 2),
        in_specs=in_specs,
        out_specs=[pl.BlockSpec((seglen, S5_HALF), lambda b, g: (seg_base + b, g)),
                   pl.BlockSpec((2, 1, 1, 1, hs2), lambda b, g: (0, b, g, 0, 0))],
        out_shape=[jax.ShapeDtypeStruct((r, S5_WIDTH), F32), jax.ShapeDtypeStruct(x0.shape, F32)],
        scratch_shapes=[pltpu.VMEM((n, kw), BF16), pltpu.VMEM((n, hs2), F32), pltpu.VMEM((n, hs2), F32)],
        input_output_aliases={6: 0},
        compiler_params=_params("parallel", "parallel"),
        name="s5_segment",
    )(z, sp['ktoep'], sp['wb'], sp['wc'], sp['a_chunk'], x0, y_prev)


def _s5_operators(p):
    f32 = F32
    cs = S5_CHUNK
    g, n, hch = S5_GROUPS, S5_STATE, S5_GROUP_CH
    kts, wbs, wcs, acs = 0.0, [], [], []
    for d in range(2):
        lam = lax.complex(jnp.minimum(p['s5_a_re'][d].astype(f32), -1e-4), p['s5_a_im'][d].astype(f32))
        a_bar = jnp.exp(lam * jnp.exp(p['s5_log_dt'][d].astype(f32))[:, None])
        b_bar = ((a_bar - 1.0) / lam)[..., None] * lax.complex(p['s5_b_re'][d].astype(f32),
                                                               p['s5_b_im'][d].astype(f32))
        cc = lax.complex(p['s5_c_re'][d].astype(f32), p['s5_c_im'][d].astype(f32))
        pw = a_bar[None] ** jnp.arange(cs + 1, dtype=f32)[:, None, None].astype(jnp.complex64)
        ktau = jnp.einsum('ghn,tgn,gni->tghi', cc, pw[:cs], b_bar).real
        ii = jnp.arange(cs)[:, None]
        jj = jnp.arange(cs)[None, :]
        lag = (ii - jj) if d == 1 else (jj - ii)
        kt = jnp.where((lag >= 0)[:, :, None, None, None], ktau[jnp.clip(lag, 0, cs - 1)], 0.0)
        kts = kts + kt
        e_in = (jnp.arange(cs) if d == 1 else (cs - 1 - jnp.arange(cs)))
        wbc = pw[e_in][:, :, :, None] * b_bar[None]
        f_out = ((cs - jnp.arange(cs)) if d == 1 else (jnp.arange(cs) + 1))
        wcc = cc[None] * pw[f_out][:, :, None, :]
        wb_h, wc_h, ac_h = [], [], []
        for half in range(2):
            gs = slice(half * 8, half * 8 + 8)
            eye8 = jnp.eye(8, dtype=f32)
            wbg = wbc[:, gs]
            wre = jnp.einsum('ignh,gk->ighkn', wbg.real, eye8).reshape(cs * S5_HALF, S5_HSTATE)
            wim = jnp.einsum('ignh,gk->ighkn', wbg.imag, eye8).reshape(cs * S5_HALF, S5_HSTATE)
            wb_h.append(jnp.concatenate([wre, wim], axis=1))
            wcg = wcc[:, gs]
            cre = jnp.einsum('jghn,gk->knjgh', wcg.real, eye8).reshape(S5_HSTATE, cs * S5_HALF)
            cim = jnp.einsum('jghn,gk->knjgh', -wcg.imag, eye8).reshape(S5_HSTATE, cs * S5_HALF)
            wc_h.append(jnp.concatenate([cre, cim], axis=0))
            ach = pw[cs][gs].reshape(1, S5_HSTATE)
            ac_h.append(jnp.concatenate([ach.real, ach.imag], axis=1))
        wbs.append(jnp.stack(wb_h))
        wcs.append(jnp.stack(wc_h))
        acs.append(jnp.stack(ac_h))
    dsk = p['s5_d'].astype(f32).reshape(g, hch)
    eye_t = jnp.eye(cs, dtype=f32)[:, :, None, None, None]
    kts = kts + eye_t * (dsk[:, :, None] * jnp.eye(hch, dtype=f32)[None])[None, None]
    kt_h = []
    for half in range(2):
        gs = slice(half * 8, half * 8 + 8)
        eye8 = jnp.eye(8, dtype=f32)
        kt_h.append(jnp.einsum('ijgoh,gk->ighjko', kts[:, :, gs], eye8).reshape(cs * S5_HALF, cs * S5_HALF))
    return dict(ktoep=jnp.stack(kt_h).astype(BF16), wb=jnp.stack(wbs).astype(BF16),
                wc=jnp.stack(wcs).astype(BF16), a_chunk=jnp.stack(acs))


def _s5_branch(z, sp, dims):
    nb, seq, ctx = dims
    r = z.shape[0]
    x0 = jnp.zeros((2, nb, 2, 1, 2 * S5_HSTATE), F32)
    y0 = jnp.zeros((r, S5_WIDTH), F32)
    y_ctx, x_ctx = _s5_segment(z, y0, x0, sp, ctx, (nb * seq) // ctx, nb)
    y, _ = _s5_segment(z, y_ctx, x_ctx, sp, seq, 0, nb)
    return y


def _merge_kernel(x_ref, m_ref, yf_ref, yr_ref, bonus_ref, gdn_ref, hf_ref, hr_ref, o_ref, s5_ref, br_ref,
                  gup_ref, gng_ref, gnb_ref, bones64_ref, glw_ref, glb_ref, mlg_ref, bones96_ref, brb_ref,
                  uprw_ref, ups5_ref, upml_ref, wout_ref, lng_ref, lnb_ref, out_ref, *, nlat, seq, nb, tm):
    d = D_MODEL
    mrow = _mod_row(pl.program_id(0) * tm, nlat, seq, nb)
    gate = m_ref[pl.ds(mrow, 1), 5 * d:6 * d]
    ys = yf_ref[...] + yr_ref[...]
    b64 = bones64_ref[...]
    yc = ys - _mm_xr(ys, b64) * (1.0 / RW_HEAD_DIM)
    var = _mm_xr(yc * yc, b64) * (1.0 / RW_HEAD_DIM)
    yn = yc * lax.rsqrt(var + RW_GN_EPS) * gng_ref[...] + gnb_ref[...]
    rw_y = (yn + bonus_ref[...]) * _mm(jax.nn.sigmoid(gdn_ref[...]), gup_ref[...])
    s5 = jax.nn.gelu(s5_ref[...])
    s5_y = s5 * jax.nn.sigmoid(_mm(s5, glw_ref[...]) + glb_ref[...])
    hg = jax.nn.sigmoid(o_ref[...]) * (hf_ref[...] + hr_ref[...])
    b96 = bones96_ref[...]
    hc = hg - _mm_xr(hg, b96) * (1.0 / ML_HEAD_DIM)
    hv = _mm_xr(hc * hc, b96) * (1.0 / ML_HEAD_DIM)
    ml_y = hc * lax.rsqrt(hv + ML_EPS) * mlg_ref[...]
    bg = jax.nn.sigmoid(br_ref[...] + brb_ref[...])
    y = (bg[:, 0:d] * _mm(rw_y, uprw_ref[...]) + bg[:, d:2 * d] * _mm(s5_y, ups5_ref[...])
         + bg[:, 2 * d:3 * d] * _mm(ml_y, upml_ref[...]))
    delta = gate * _mm(y, wout_ref[...])
    out_ref[...] = _layer_norm(DEEPNORM_ALPHA * x_ref[...] + delta, lng_ref[...], lnb_ref[...])


def _merge(xs, mods, yf, yr, bonus, hf, hr, s5y, z, p, dims):
    nb, seq, ctx = dims
    r, d = xs.shape
    nlat = nb * seq
    tm = _pick_tile((512, 256, 128, 64), nlat, nb * ctx)
    w = RW_WIDTH
    row = lambda width, col=0: pl.BlockSpec((tm, width), lambda i: (i, col))
    full = lambda a: pl.BlockSpec(a.shape, lambda i: (0,) * a.ndim)
    consts = [p['g_up'], p['gn_g'], p['gn_b'], p['bones64'], p['glu_w'], p['glu_b'], p['ml_norm_g'], p['bones96'],
              p['br_b'], p['up_rw'], p['up_s5'], p['up_ml'], p['w_out'], p['ln_g1'], p['ln_b1']]
    kern = functools.partial(_merge_kernel, nlat=nlat, seq=seq, nb=nb, tm=tm)
    return pl.pallas_call(
        kern,
        grid=(r // tm,),
        in_specs=[row(d), pl.BlockSpec(mods.shape, lambda i: (0, 0)),
                  row(w), row(w), row(w), row(RW_GATE_LORA, Z_GDN // RW_GATE_LORA),
                  row(w), row(w), row(w, Z_MLO // w), row(S5_WIDTH), row(3 * d, 0)] + [full(a) for a in consts],
        out_specs=row(d),
        out_shape=jax.ShapeDtypeStruct((r, d), F32),
        compiler_params=_params("parallel"),
        name="merge",
    )(xs, mods, yf, yr, bonus, z, hf, hr, z, s5y, z, *consts)


def _in_proj_perm():
    old = {}
    start = 0
    for name, size in (('conv', CONV_COLS), ('ml_v', ML_WIDTH), ('ml_o', ML_WIDTH), ('ml_gl', 4 * ML_HEADS),
                       ('s5_u', S5_WIDTH), ('w_dn', RW_DECAY_LORA), ('a_dn', RW_AAA_LORA), ('g_dn', RW_GATE_LORA),
                       ('br_gl', N_BRANCH * D_MODEL)):
        old[name] = np.arange(start, start + size)
        start += size
    idx = np.zeros((Z_WIDTH,), np.int32)
    valid = np.zeros((Z_WIDTH,), np.float32)
    for name, off in (('br_gl', Z_BR), ('conv', Z_CONV), ('ml_v', Z_MLV), ('ml_o', Z_MLO), ('w_dn', Z_LORA),
                      ('a_dn', Z_LORA + RW_DECAY_LORA), ('s5_u', Z_S5), ('g_dn', Z_GDN), ('ml_gl', Z_MLG)):
        n = old[name].shape[0]
        idx[off:off + n] = old[name]
        valid[off:off + n] = 1.0
    return idx, valid


def _block_ones(width, group):
    g = np.arange(width) // group
    return (g[:, None] == g[None, :]).astype(np.float32)


def _layer_params(i, a):
    w = RW_WIDTH
    pad_lo = jnp.zeros((2, RW_AAA_LORA, w), F32)
    gb = a['ml_gate_b'][i].reshape(16).astype(F32)
    return dict(
        k_k=a['rw_k_k'][i].reshape(1, w), k_a=a['rw_k_a'][i].reshape(1, w), r_k=a['rw_r_k'][i].reshape(1, w),
        w0=a['rw_w0'][i].reshape(2, 1, w), a0=a['rw_a0'][i].reshape(2, 1, w),
        w_up=jnp.concatenate([a['rw_w_up'][i], pad_lo], axis=1).astype(BF16),
        a_up=jnp.concatenate([pad_lo, a['rw_a_up'][i]], axis=1).astype(BF16),
        bones64=jnp.asarray(_block_ones(w, RW_HEAD_DIM), BF16),
        bones96=jnp.asarray(_block_ones(ML_WIDTH, ML_HEAD_DIM), BF16),
        bmask96=jnp.asarray(_block_ones(ML_WIDTH, ML_HEAD_DIM), F32),
        gb_row=jnp.concatenate([gb, jnp.zeros((LANES - 16,), F32)]).reshape(1, LANES), gb_col=gb.reshape(16, 1),
        g_up=a['rw_g_up'][i].astype(BF16), gn_g=a['rw_gn_g'][i].reshape(1, w), gn_b=a['rw_gn_b'][i].reshape(1, w),
        glu_w=a['s5_glu_w'][i].astype(BF16), glu_b=a['s5_glu_b'][i].reshape(1, S5_WIDTH),
        ml_norm_g=a['ml_norm_g'][i].reshape(1, ML_WIDTH), br_b=a['br_gate_b'][i].reshape(1, N_BRANCH * D_MODEL),
        up_rw=a['up_rw'][i].astype(BF16), up_s5=a['up_s5'][i].astype(BF16), up_ml=a['up_ml'][i].astype(BF16),
        w_out=a['w_out'][i].astype(BF16),
        ln_g1=a['ln_g'][i, 1].reshape(1, D_MODEL), ln_b1=a['ln_b'][i, 1].reshape(1, D_MODEL),
    )


def kernel(x, c, ctx, c_ctx, ada_w, ada_b, ln_g, ln_b, ffn_w_gate, ffn_w_up, ffn_w_down, w_in, conv_w, rw_w0, rw_w_up, rw_a0, rw_a_up, rw_g_up, rw_k_k, rw_k_a, rw_r_k, rw_gn_g, rw_gn_b, s5_a_re, s5_a_im, s5_log_dt, s5_b_re, s5_b_im, s5_c_re, s5_c_im, s5_d, s5_glu_w, s5_glu_b, ml_gate_b, ml_norm_g, up_rw, up_s5, up_ml, br_gate_b, w_out):
    a = dict(ln_g=ln_g, ln_b=ln_b, rw_w0=rw_w0, rw_w_up=rw_w_up, rw_a0=rw_a0, rw_a_up=rw_a_up, rw_g_up=rw_g_up,
             rw_k_k=rw_k_k, rw_k_a=rw_k_a, rw_r_k=rw_r_k, rw_gn_g=rw_gn_g, rw_gn_b=rw_gn_b, s5_glu_w=s5_glu_w,
             s5_glu_b=s5_glu_b, ml_gate_b=ml_gate_b, ml_norm_g=ml_norm_g, up_rw=up_rw, up_s5=up_s5, up_ml=up_ml,
             br_gate_b=br_gate_b, w_out=w_out)
    nb, seq, d = x.shape
    nctx = ctx.shape[1]
    dims = (nb, seq, nctx)
    depth = ada_w.shape[0]
    rows = -(-(nb + 1) // SUBLANES) * SUBLANES
    cv = jnp.concatenate([c, c_ctx[None, :], jnp.zeros((rows - nb - 1, d), F32)], axis=0)
    mods = _modulation(cv, ada_w, ada_b)
    xs = jnp.concatenate([x.reshape(nb * seq, d), ctx.reshape(nb * nctx, d)], axis=0)
    perm, valid = _in_proj_perm()
    for i in range(depth):
        m = mods[i]
        p = _layer_params(i, a)
        sp = _s5_operators(dict(s5_a_re=s5_a_re[i], s5_a_im=s5_a_im[i], s5_log_dt=s5_log_dt[i], s5_b_re=s5_b_re[i],
                                s5_b_im=s5_b_im[i], s5_c_re=s5_c_re[i], s5_c_im=s5_c_im[i], s5_d=s5_d[i]))
        w_in_p = (jnp.take(w_in[i], perm, axis=1) * valid[None, :]).astype(BF16)
        xs = _ffn_half(xs, m, 0, ffn_w_gate[i, 0].astype(BF16), ffn_w_up[i, 0].astype(BF16),
                       ffn_w_down[i, 0].astype(BF16), ln_g[i, 0], ln_b[i, 0], dims)
        z = _in_proj(xs, m, w_in_p, dims)
        zc = _short_conv(z, conv_w[i], dims)
        y1, rq, pt, qt, bonus = _rwkv_chunk(zc, z, p, dims)
        yf, yr = _rwkv_scan(y1, rq, pt, qt, dims)
        glt = z[:, Z_MLG:Z_MLG + 16].reshape(-1, CHUNK, 16).transpose(0, 2, 1)
        hf, hr = _mlstm(zc, z, glt, p, dims)
        s5y = _s5_branch(z, sp, dims)
        xs = _merge(xs, m, yf, yr, bonus, hf, hr, s5y, z, p, dims)
        xs = _ffn_half(xs, m, 6, ffn_w_gate[i, 1].astype(BF16), ffn_w_up[i, 1].astype(BF16),
                       ffn_w_down[i, 1].astype(BF16), ln_g[i, 2], ln_b[i, 2], dims)
    return xs[:nb * seq].reshape(nb, seq, d)
```

```python
import functools
import math

import numpy as np
import jax
import jax.numpy as jnp
from jax import lax
from jax.experimental import pallas as pl
from jax.experimental.pallas import tpu as pltpu

F32 = jnp.float32
BF16 = jnp.bfloat16

D_MODEL = 1024
DEPTH = 2
GRID_W = 64

RW_HEADS = 6
RW_HEAD_DIM = 64
RW_WIDTH = 384
RW_DECAY_LORA = 64
RW_AAA_LORA = 64
RW_GATE_LORA = 128
RW_GN_EPS = 64e-5

S5_GROUPS = 16
S5_GROUP_CH = 16
S5_WIDTH = 256
S5_STATE = 64

ML_HEADS = 4
ML_HEAD_DIM = 96
ML_WIDTH = 384
ML_EPS = 1e-5

D_FF = 2816
N_BRANCH = 3
N_MOD = 9
LN_EPS = 1e-5
DEEPNORM_ALPHA = (2.0 * DEPTH) ** 0.25

CONV_COLS = 3 * RW_WIDTH + 2 * ML_WIDTH

LANES = 128
SUBLANES = 8
CHUNK = 64
S5_CHUNK = 8
S5_HALF = 128
S5_HSTATE = 512
FFN_COLS = 256

Z_BR = 0
Z_CONV = 3072
Z_MLV = Z_CONV + CONV_COLS
Z_MLO = Z_MLV + ML_WIDTH
Z_LORA = Z_MLO + ML_WIDTH
Z_S5 = Z_LORA + 128
Z_GDN = Z_S5 + S5_WIDTH
Z_MLG = Z_GDN + RW_GATE_LORA
Z_WIDTH = Z_MLG + 128

VMEM_LIMIT = 56 * 1024 * 1024

_NN = (((1,), (0,)), ((), ()))
_NT = (((1,), (1,)), ((), ()))
_TN = (((0,), (0,)), ((), ()))


def _dot(a, b, dims=_NN):
    return lax.dot_general(a, b, dims, preferred_element_type=F32)


def _mm(a, b, dims=_NN):
    return _dot(a.astype(BF16), b.astype(BF16), dims)


def _split(a, n):
    parts, r = [], a
    for _ in range(n):
        p = r.astype(BF16)
        parts.append(p)
        r = r - p.astype(F32)
    return parts


def _mm_xr(a, b_exact, dims=_NN):
    ps = _split(a, 3)
    return _dot(ps[0], b_exact, dims) + (_dot(ps[1], b_exact, dims) + _dot(ps[2], b_exact, dims))


def _mm_xl(a_exact, b, dims=_NN):
    ps = _split(b, 3)
    return _dot(a_exact, ps[0], dims) + (_dot(a_exact, ps[1], dims) + _dot(a_exact, ps[2], dims))


def _mm3(a, b, dims=_NN):
    ah, al = _split(a, 2)
    bh, bl = _split(b, 2)
    return _dot(ah, bh, dims) + (_dot(al, bh, dims) + _dot(ah, bl, dims))


def _softplus(x):
    return jnp.maximum(x, 0.0) + jnp.log1p(jnp.exp(-jnp.abs(x)))


def _layer_norm(y, g, b):
    yc = y - jnp.mean(y, axis=-1, keepdims=True)
    return yc * lax.rsqrt(jnp.mean(yc * yc, axis=-1, keepdims=True) + LN_EPS) * g + b


def _pick_tile(cands, *lens):
    for t in cands:
        if all(n % t == 0 for n in lens):
            return t
    raise ValueError(f"no tile in {cands} divides {lens}")


def _params(*sem):
    return pltpu.CompilerParams(dimension_semantics=sem, vmem_limit_bytes=VMEM_LIMIT)


def _mod_row(row0, nlat, seq, nb):
    return jnp.where(row0 < nlat, row0 // seq, nb)


def _mod_kernel(cv_ref, w_ref, b_ref, o_ref):
    s = cv_ref[...]
    s = s * jax.nn.sigmoid(s)
    o_ref[0] = _mm3(s, w_ref[0]) + b_ref[0]


def _modulation(cv, ada_w, ada_b):
    depth, d, nd = ada_w.shape
    rows = cv.shape[0]
    return pl.pallas_call(
        _mod_kernel,
        grid=(depth, nd // d),
        in_specs=[pl.BlockSpec((rows, d), lambda l, j: (0, 0)),
                  pl.BlockSpec((1, d, d), lambda l, j: (l, 0, j)),
                  pl.BlockSpec((1, 1, d), lambda l, j: (l, 0, j))],
        out_specs=pl.BlockSpec((1, rows, d), lambda l, j: (l, 0, j)),
        out_shape=jax.ShapeDtypeStruct((depth, rows, nd), F32),
        compiler_params=_params("parallel", "parallel"),
        name="modulation",
    )(cv, ada_w, ada_b.reshape(depth, 1, nd))


def _ffn_kernel(x_ref, m_ref, wg_ref, wu_ref, wd_ref, g_ref, b_ref, o_ref, h_scr, *, k0, nlat, seq, nb, tm):
    d = D_MODEL
    mrow = _mod_row(pl.program_id(0) * tm, nlat, seq, nb)
    shift = m_ref[pl.ds(mrow, 1), k0 * d:(k0 + 1) * d]
    scale = m_ref[pl.ds(mrow, 1), (k0 + 1) * d:(k0 + 2) * d]
    gate = m_ref[pl.ds(mrow, 1), (k0 + 2) * d:(k0 + 3) * d]
    u = (x_ref[...] * (1.0 + scale) + shift).astype(BF16)
    for f in range(D_FF // FFN_COLS):
        cols = slice(f * FFN_COLS, (f + 1) * FFN_COLS)
        g = _dot(u, wg_ref[:, cols])
        up = _dot(u, wu_ref[:, cols])
        h_scr[:, cols] = (g * jax.nn.sigmoid(g) * up).astype(BF16)
    y = DEEPNORM_ALPHA * x_ref[...] + (0.5 * gate) * _dot(h_scr[...], wd_ref[...])
    o_ref[...] = _layer_norm(y, g_ref[...], b_ref[...])


def _ffn_half(xs, mods, k0, wg, wu, wd, g, b, dims):
    nb, seq, ctx = dims
    r, d = xs.shape
    nlat = nb * seq
    tm = _pick_tile((1024, 512, 256, 128, 64), nlat, nb * ctx)
    kern = functools.partial(_ffn_kernel, k0=k0, nlat=nlat, seq=seq, nb=nb, tm=tm)
    resident = lambda shape: pl.BlockSpec(shape, lambda i: (0, 0), pipeline_mode=pl.Buffered(1))
    return pl.pallas_call(
        kern,
        grid=(r // tm,),
        in_specs=[pl.BlockSpec((tm, d), lambda i: (i, 0)),
                  resident(mods.shape), resident((d, D_FF)), resident((d, D_FF)), resident((D_FF, d)),
                  resident((1, d)), resident((1, d))],
        out_specs=pl.BlockSpec((tm, d), lambda i: (i, 0)),
        out_shape=jax.ShapeDtypeStruct((r, d), F32),
        scratch_shapes=[pltpu.VMEM((tm, D_FF), BF16)],
        compiler_params=_params("parallel"),
        name="ffn_half",
    )(xs, mods, wg, wu, wd, g.reshape(1, d), b.reshape(1, d))


def _inproj_kernel(x_ref, m_ref, w_ref, o_ref, u_scr, *, nlat, seq, nb, tm):
    d = D_MODEL
    mrow = _mod_row(pl.program_id(0) * tm, nlat, seq, nb)

    @pl.when(pl.program_id(1) == 0)
    def _():
        shift = m_ref[pl.ds(mrow, 1), 3 * d:4 * d]
        scale = m_ref[pl.ds(mrow, 1), 4 * d:5 * d]
        u_scr[...] = (x_ref[...] * (1.0 + scale) + shift).astype(BF16)

    o_ref[...] = _dot(u_scr[...], w_ref[...])


def _in_proj(xs, mods, w_in_p, dims):
    nb, seq, ctx = dims
    r, d = xs.shape
    nlat = nb * seq
    tm = _pick_tile((1024, 512, 256, 128, 64), nlat, nb * ctx)
    tn = 1280
    kern = functools.partial(_inproj_kernel, nlat=nlat, seq=seq, nb=nb, tm=tm)
    return pl.pallas_call(
        kern,
        grid=(r // tm, Z_WIDTH // tn),
        in_specs=[pl.BlockSpec((tm, d), lambda i, j: (i, 0)),
                  pl.BlockSpec(mods.shape, lambda i, j: (0, 0)),
                  pl.BlockSpec((d, tn), lambda i, j: (0, j))],
        out_specs=pl.BlockSpec((tm, tn), lambda i, j: (i, j)),
        out_shape=jax.ShapeDtypeStruct((r, Z_WIDTH), F32),
        scratch_shapes=[pltpu.VMEM((tm, d), BF16)],
        compiler_params=_params("parallel", "arbitrary"),
        name="in_proj",
    )(xs, mods, w_in_p)


def _conv_kernel(prev_ref, cur_ref, next_ref, w_ref, o_ref, *, tmc, nlat, seq, ctx):
    row0 = pl.program_id(0) * tmc
    is_lat = row0 < nlat
    off = jnp.where(is_lat, lax.rem(row0, seq), lax.rem(jnp.maximum(row0 - nlat, 0), ctx))
    has_prev = jnp.logical_and(is_lat, off > 0)
    has_next = jnp.logical_and(is_lat, off + tmc < seq)
    pos = lax.broadcasted_iota(jnp.int32, (tmc, 1), 0)
    col = lax.rem(pos, GRID_W)
    keep = lambda drop: jnp.where(drop, 0.0, 1.0)
    m_left = jnp.where(is_lat, keep(col == 0), keep(jnp.logical_and(pos == 0, off == 0)))
    m_right = jnp.where(is_lat, keep(col == GRID_W - 1),
                        keep(jnp.logical_and(pos == tmc - 1, off + tmc == ctx)))
    cur = cur_ref[...]
    prev = jnp.where(has_prev, prev_ref[...], 0.0)
    nxt = jnp.where(has_next, next_ref[...], 0.0)
    if tmc > GRID_W:
        up = jnp.concatenate([prev, cur[:tmc - GRID_W]], axis=0)
        dn = jnp.concatenate([cur[GRID_W:], nxt], axis=0)
    else:
        up, dn = prev, nxt
    lat_f = jnp.where(is_lat, 1.0, 0.0)
    w = w_ref[...]
    w_up = w[0] * lat_f
    w_dn = w[2] * lat_f

    def tap(j):
        return up * w_up[j:j + 1, :] + cur * w[1, j:j + 1, :] + dn * w_dn[j:j + 1, :]

    o_ref[...] = (m_left * pltpu.roll(tap(0), 1, 0) + tap(1) + m_right * pltpu.roll(tap(2), tmc - 1, 0))


def _short_conv(z, conv_w, dims):
    nb, seq, ctx = dims
    r = z.shape[0]
    nlat = nb * seq
    tmc = _pick_tile((256, 128, 64), seq, ctx)
    cw = RW_WIDTH
    hb = tmc // GRID_W
    nblk = r // GRID_W
    c0 = Z_CONV // cw
    kern = functools.partial(_conv_kernel, tmc=tmc, nlat=nlat, seq=seq, ctx=ctx)
    return pl.pallas_call(
        kern,
        grid=(r // tmc, CONV_COLS // cw),
        in_specs=[pl.BlockSpec((GRID_W, cw), lambda i, c: (jnp.maximum(i * hb - 1, 0), c0 + c)),
                  pl.BlockSpec((tmc, cw), lambda i, c: (i, c0 + c)),
                  pl.BlockSpec((GRID_W, cw), lambda i, c: (jnp.minimum(i * hb + hb, nblk - 1), c0 + c)),
                  pl.BlockSpec((3, 3, cw), lambda i, c: (0, 0, c))],
        out_specs=pl.BlockSpec((tmc, cw), lambda i, c: (i, c)),
        out_shape=jax.ShapeDtypeStruct((r, CONV_COLS), F32),
        compiler_params=_params("parallel", "parallel"),
        name="short_conv",
    )(z, z, z, conv_w)


def _chunk_block(b, s, rev, nb, nc, nl):
    cc = (nc - 1 - s) if rev else s
    lc = (nl - 1 - (s - nc)) if rev else (s - nc)
    return jnp.where(s < nc, nb * nl + b * nc + cc, b * nl + lc)


def _pair_masks():
    lane = lax.broadcasted_iota(jnp.int32, (1, LANES), 1)
    m0 = (lane < RW_HEAD_DIM).astype(F32)
    return m0, 1.0 - m0


def _bd(y, m0, m1):
    n = y.shape[1] // LANES
    if n > 1:
        m0 = jnp.concatenate([m0] * n, axis=1)
        m1 = jnp.concatenate([m1] * n, axis=1)
    return jnp.concatenate([y * m0, y * m1], axis=0)


def _rwkv_chunk_kernel(r_ref, k_ref, v_ref, lo_ref, kk_ref, ka_ref, rk_ref, w0_ref, wup_ref, a0_ref, aup_ref,
                       bones_ref, y1_ref, rq_ref, pt_ref, qt_ref, bonus_ref, *, nch):
    c = CHUNK
    bones = bones_ref[...]
    m0, m1 = _pair_masks()
    m0h, m1h = m0.astype(BF16), m1.astype(BF16)

    def bd(y):
        y = y.astype(BF16)
        n = y.shape[1] // LANES
        a0 = jnp.concatenate([m0h] * n, axis=1) if n > 1 else m0h
        a1 = jnp.concatenate([m1h] * n, axis=1) if n > 1 else m1h
        return jnp.concatenate([y * a0, y * a1], axis=0)

    ti = lax.broadcasted_iota(jnp.int32, (c, c), 0)
    tj = lax.broadcasted_iota(jnp.int32, (c, c), 1)
    pt_i = lax.broadcasted_iota(jnp.int32, (c, LANES), 0)
    pt_j = lax.rem(lax.broadcasted_iota(jnp.int32, (c, LANES), 1), c)
    eye_p = (pt_i == pt_j).astype(F32)
    masks = []
    for d in range(2):
        strict = ((pt_j > pt_i) if d == 1 else (pt_j < pt_i)).astype(F32)
        masks.append((((tj >= ti) if d == 1 else (tj <= ti)).astype(BF16), strict, strict + eye_p))

    chains = []
    for q in range(nch):
        rows = slice(q * c, (q + 1) * c)
        r = r_ref[rows, :]
        k = k_ref[rows, :]
        v = v_ref[rows, :]
        lo = lo_ref[rows, :]
        kk = k * kk_ref[...]
        kk = kk * lax.rsqrt(jnp.maximum(_mm_xr(kk * kk, bones), 1e-24))
        tw = jnp.tanh(lo)
        bonus = jnp.zeros_like(r)
        for d in range(2):
            rev = d == 1
            tri, strict, incl = masks[d]
            wl = w0_ref[d] + _mm(tw, wup_ref[d])
            lw = -jnp.exp(-_softplus(-wl) - 0.5)
            a = jax.nn.sigmoid(a0_ref[d] + _mm(lo, aup_ref[d]))
            kd = k * (1.0 + (a - 1.0) * ka_ref[...])
            bb = kk * a
            bonus = bonus + _mm_xr(r * kd * rk_ref[...], bones) * v
            cs = _mm_xl(tri, lw)
            csp = cs - lw
            tot = cs[0:1, :] if rev else cs[c - 1:c, :]
            mid = 0.5 * tot
            e_dn = jnp.exp(mid - cs)
            a_hat = kk * jnp.exp(csp - mid)
            r_hat = r * jnp.exp(cs - mid)
            b_hat = bb * e_dn
            k_hat = kd * e_dn
            a_st = kk * jnp.exp(csp)
            r_st = r * jnp.exp(cs)
            e_end = jnp.exp(tot - cs)
            b_end = bb * e_end
            k_end = kd * e_end
            w_c = jnp.exp(tot)
            for p in range(RW_HEADS // 2):
                sl = slice(p * LANES, (p + 1) * LANES)
                chains.append(dict(d=d, rows=rows, sl=sl, strict=strict, incl=incl, vp=v[:, sl], a_hat=a_hat[:, sl],
                                   r_hat=r_hat[:, sl], b_hat=b_hat[:, sl], k_hat=k_hat[:, sl], a_st=a_st[:, sl],
                                   r_st=r_st[:, sl], b_end=b_end[:, sl], k_end=k_end[:, sl], w_c=w_c[:, sl]))
        bonus_ref[rows, :] = bonus

    for ch in chains:
        lhs = jnp.concatenate([ch['a_hat'], ch['r_hat']], axis=0).astype(BF16)
        rhs_t = jnp.concatenate([bd(ch['b_hat']), bd(ch['k_hat'])], axis=0)
        g4 = _dot(lhs, rhs_t, _NT)
        ch['lp'] = g4[0:c, 0:LANES] * ch['strict']
        ch['t'] = eye_p - ch['lp']
        ch['m_rb'] = (g4[c:, 0:LANES] * ch['incl']).astype(BF16)
        ch['lak_mrk'] = jnp.concatenate([g4[0:c, LANES:] * ch['strict'], g4[c:, LANES:] * ch['incl']],
                                        axis=0).astype(BF16)
    for ch in chains:
        ch['lp'] = _dot(ch['lp'].astype(BF16), bd(ch['lp']))
    for it in range(5):
        for ch in chains:
            if it < 4:
                res = _dot(jnp.concatenate([ch['t'], ch['lp']], axis=0).astype(BF16), bd(ch['lp']))
                ch['t'] = ch['t'] + res[0:c]
                ch['lp'] = res[c:]
            else:
                ch['t'] = ch['t'] + _dot(ch['t'].astype(BF16), bd(ch['lp']))
    for ch in chains:
        xv = _dot(ch['lak_mrk'], bd(ch['vp']))
        ch['x0'] = xv[0:c]
        ch['mrk_v'] = xv[c:]
    for ch in chains:
        ch['ua'] = _dot(ch['t'].astype(BF16), bd(jnp.concatenate([ch['x0'], ch['a_st']], axis=1)))
    for ch in chains:
        d, rows, sl = ch['d'], ch['rows'], ch['sl']
        mu = _dot(ch['m_rb'], bd(ch['ua']))
        y1_ref[d, rows, sl] = ch['mrk_v'] - mu[:, 0:LANES]
        rq_ref[d, rows, sl] = ch['r_st'] - mu[:, LANES:]
    for ch in chains:
        d, rows, sl = ch['d'], ch['rows'], ch['sl']
        ua = ch['ua']
        rhs = jnp.concatenate([jnp.concatenate([ch['vp'], jnp.zeros_like(ch['vp'])], axis=1),
                               jnp.concatenate([-ua[:, 0:LANES], ua[:, LANES:]], axis=1)], axis=0)
        hg = _mm(jnp.concatenate([ch['k_end'], ch['b_end']], axis=0), rhs, _TN)
        qt_ref[d, rows, sl] = hg[0:c, 0:LANES] * m0 + hg[c:, 0:LANES] * m1
        pt_ref[d, rows, sl] = eye_p * ch['w_c'] - (hg[0:c, LANES:] * m0 + hg[c:, LANES:] * m1)


def _rwkv_chunk(zc, z, p, dims):
    r = zc.shape[0]
    w = RW_WIDTH
    nch = 2 if (r // CHUNK) % 2 == 0 else 1
    c = nch * CHUNK
    full = lambda shape: pl.BlockSpec(shape, lambda i: (0,) * len(shape))
    out4 = jax.ShapeDtypeStruct((2, r, w), F32)
    return pl.pallas_call(
        functools.partial(_rwkv_chunk_kernel, nch=nch),
        grid=(r // c,),
        in_specs=[pl.BlockSpec((c, w), lambda i: (i, 0)),
                  pl.BlockSpec((c, w), lambda i: (i, 1)),
                  pl.BlockSpec((c, w), lambda i: (i, 2)),
                  pl.BlockSpec((c, LANES), lambda i: (i, Z_LORA // LANES)),
                  full((1, w)), full((1, w)), full((1, w)),
                  full((2, 1, w)), full((2, LANES, w)), full((2, 1, w)), full((2, LANES, w)),
                  full((w, w))],
        out_specs=[pl.BlockSpec((2, c, w), lambda i: (0, i, 0))] * 4 + [pl.BlockSpec((c, w), lambda i: (i, 0))],
        out_shape=[out4, out4, out4, out4, jax.ShapeDtypeStruct((r, w), F32)],
        compiler_params=_params("parallel"),
        name="rwkv_chunk",
    )(zc, zc, zc, z, p['k_k'], p['k_a'], p['r_k'], p['w0'], p['w_up'], p['a0'], p['a_up'], p['bones64'])


def _rwkv_scan_kernel(y1f, rqf, ptf, qtf, y1r, rqr, ptr, qtr, yf_ref, yr_ref, st_scr):
    c = CHUNK

    @pl.when(pl.program_id(1) == 0)
    def _():
        st_scr[...] = jnp.zeros_like(st_scr)

    m0, m1 = _pair_masks()
    for d, (y1, rq, pt, qt, yo) in enumerate(((y1f, rqf, ptf, qtf, yf_ref), (y1r, rqr, ptr, qtr, yr_ref))):
        for p in range(RW_HEADS // 2):
            sl = slice(p * LANES, (p + 1) * LANES)
            st = st_scr[d, :, sl]
            lhs = jnp.concatenate([rq[0, :, sl], pt[0, :, sl]], axis=0)
            res = _mm3(lhs, _bd(st, m0, m1))
            yo[:, sl] = y1[0, :, sl] + res[0:c]
            st_scr[d, :, sl] = res[c:] + qt[0, :, sl]


def _rwkv_scan(y1, rq, pt, qt, dims):
    nb, seq, ctx = dims
    r = y1.shape[1]
    w = RW_WIDTH
    c = CHUNK
    nc, nl = ctx // c, seq // c

    def spec(d):
        return pl.BlockSpec((1, c, w), lambda b, s: (d, _chunk_block(b, s, d == 1, nb, nc, nl), 0))

    def ospec(d):
        return pl.BlockSpec((c, w), lambda b, s: (_chunk_block(b, s, d == 1, nb, nc, nl), 0))

    out = jax.ShapeDtypeStruct((r, w), F32)
    return pl.pallas_call(
        _rwkv_scan_kernel,
        grid=(nb, nc + nl),
        in_specs=[spec(0)] * 4 + [spec(1)] * 4,
        out_specs=[ospec(0), ospec(1)],
        out_shape=[out, out],
        scratch_shapes=[pltpu.VMEM((2, c, w), F32)],
        compiler_params=_params("parallel", "arbitrary"),
        name="rwkv_scan",
    )(y1, rq, pt, qt, y1, rq, pt, qt)


def _mlstm_kernel(qf, kf, vf, gcf, gtf, qr, kr, vr, gcr, gtr, gbr_ref, gbc_ref, bmask_ref,
                  hf_ref, hr_ref, c_scr, n_scr, m_scr):
    c = CHUNK
    nh, dh = ML_HEADS, ML_HEAD_DIM

    @pl.when(pl.program_id(1) == 0)
    def _():
        c_scr[...] = jnp.zeros_like(c_scr)
        n_scr[...] = jnp.zeros_like(n_scr)
        m_scr[...] = jnp.zeros_like(m_scr)

    lane = lax.broadcasted_iota(jnp.int32, (1, ML_WIDTH), 1)
    hmask = [jnp.logical_and(lane >= h * dh, lane < (h + 1) * dh).astype(F32) for h in range(nh)]
    lane1 = lax.broadcasted_iota(jnp.int32, (1, LANES), 1)
    ti = lax.broadcasted_iota(jnp.int32, (c, c), 0)
    tj = lax.broadcasted_iota(jnp.int32, (c, c), 1)
    bmask = bmask_ref[...]

    for d, (q_ref, k_ref, v_ref, gc_ref, gt_ref, h_ref) in enumerate(
            ((qf, kf, vf, gcf, gtf, hf_ref), (qr, kr, vr, gcr, gtr, hr_ref))):
        rev = d == 1
        valid = (tj >= ti) if rev else (tj <= ti)
        tri = valid.astype(BF16)
        q = q_ref[...]
        q = q * jax.nn.sigmoid(q)
        k = k_ref[...]
        k = k * jax.nn.sigmoid(k) * (dh ** -0.5)
        v = v_ref[...]
        gc = gc_ref[...] + gbr_ref[...]
        gt = gt_ref[0] + gbc_ref[...]
        fc = -_softplus(-gc)
        ft = -_softplus(-gt)
        bc_cols = _mm_xl(tri, fc)
        bc_rows = _mm_xr(ft, tri, _NT)
        c_prev = c_scr[d]
        n_prev = n_scr[d]
        m_all = m_scr[d]
        qc = _mm(q, c_prev)
        s_parts, inter_e, den_parts, wgt_e, dec_e, m_new_all = [], 0.0, [], 0.0, 0.0, 0.0
        for h in range(nh):
            ci, cf = d * 8 + h, d * 8 + 4 + h
            bcc = bc_cols[:, cf:cf + 1]
            icc = gc[:, ci:ci + 1]
            bcr = bc_rows[cf:cf + 1, :]
            icr = gt[ci:ci + 1, :]
            b_last = bcc[0:1, :] if rev else bcc[c - 1:c, :]
            m_prev = m_all[:, h:h + 1]
            logd = jnp.where(valid, bcc - bcr + icr, -jnp.inf)
            log_inter = bcc + m_prev
            m = jnp.maximum(log_inter, jnp.max(logd, axis=-1, keepdims=True))
            inter = jnp.exp(log_inter - m)
            sm = _mm(q * hmask[h], k, _NT) * jnp.exp(logd - m)
            s_parts.append(sm)
            nq = jnp.sum(q * n_prev * hmask[h], axis=-1, keepdims=True)
            den = inter * nq + jnp.sum(sm, axis=-1, keepdims=True)
            den_parts.append(jnp.maximum(jnp.abs(den), jnp.exp(-m)) * hmask[h])
            inter_e = inter_e + inter * hmask[h]
            log_w = b_last - bcc + icc
            m_new = jnp.maximum(b_last + m_prev, jnp.max(log_w, axis=0, keepdims=True))
            wgt_e = wgt_e + jnp.exp(log_w - m_new) * hmask[h]
            dec_e = dec_e + jnp.exp(b_last + m_prev - m_new) * hmask[h]
            m_new_all = m_new_all + m_new * (lane1 == h).astype(F32)
        s_all = jnp.concatenate(s_parts, axis=1)
        v_bd = jnp.concatenate([v * hmask[h] for h in range(nh)], axis=0)
        num = inter_e * qc + _mm(s_all, v_bd)
        den_e = den_parts[0] + den_parts[1] + den_parts[2] + den_parts[3]
        h_ref[...] = num / den_e
        kw = k * wgt_e
        c_scr[d] = dec_e * c_prev + bmask * _mm(kw, v, _TN)
        n_scr[d] = dec_e * n_prev + jnp.sum(kw, axis=0, keepdims=True)
        m_scr[d] = m_new_all


def _mlstm(zc, z, glt, p, dims):
    nb, seq, ctx = dims
    r = zc.shape[0]
    w = ML_WIDTH
    c = CHUNK
    nc, nl = ctx // c, seq // c

    def blk(d):
        return lambda b, s: _chunk_block(b, s, d == 1, nb, nc, nl)

    def specs(d):
        f = blk(d)
        return [pl.BlockSpec((c, w), lambda b, s: (f(b, s), 3)),
                pl.BlockSpec((c, w), lambda b, s: (f(b, s), 4)),
                pl.BlockSpec((c, w), lambda b, s: (f(b, s), Z_MLV // w)),
                pl.BlockSpec((c, LANES), lambda b, s: (f(b, s), Z_MLG // LANES)),
                pl.BlockSpec((1, 16, c), lambda b, s: (f(b, s), 0, 0))]

    full = lambda shape: pl.BlockSpec(shape, lambda b, s: (0,) * len(shape))
    out = jax.ShapeDtypeStruct((r, w), F32)
    return pl.pallas_call(
        _mlstm_kernel,
        grid=(nb, nc + nl),
        in_specs=specs(0) + specs(1) + [full((1, LANES)), full((16, 1)), full((w, w))],
        out_specs=[pl.BlockSpec((c, w), lambda b, s: (blk(0)(b, s), 0)),
                   pl.BlockSpec((c, w), lambda b, s: (blk(1)(b, s), 0))],
        out_shape=[out, out],
        scratch_shapes=[pltpu.VMEM((2, w, w), F32), pltpu.VMEM((2, 1, w), F32), pltpu.VMEM((2, 1, LANES), F32)],
        compiler_params=_params("parallel", "arbitrary"),
        name="mlstm",
    )(zc, zc, z, z, glt, zc, zc, z, z, glt, p['gb_row'], p['gb_col'], p['bmask96'])


def _s5_kernel(u_ref, kt_ref, wb_ref, wc_ref, a_ref, x0_ref, yin_ref, y_ref, xf_ref, lhs_scr, bc_scr, xp_scr,
               *, n):
    del yin_ref
    cs = S5_CHUNK
    hs = S5_HSTATE
    for i in range(cs):
        lhs_scr[:, i * S5_HALF:(i + 1) * S5_HALF] = u_ref[pl.ds(i, n, stride=cs), :].astype(BF16)
    lhs = lhs_scr[...]
    yacc = _dot(lhs, kt_ref[0])
    for d in range(2):
        bc_scr[...] = _dot(lhs, wb_ref[d, 0])
        a = a_ref[d, 0]
        ar, ai = a[:, :hs], a[:, hs:]
        x0 = x0_ref[d, 0, 0]

        def body(t, x, d=d, ar=ar, ai=ai):
            idx = (n - 1 - t) if d == 1 else t
            xr, xi = x
            xp_scr[pl.ds(idx, 1), :] = jnp.concatenate([xr, xi], axis=1)
            row = bc_scr[pl.ds(idx, 1), :]
            return (ar * xr - ai * xi + row[:, :hs], ar * xi + ai * xr + row[:, hs:])

        xr, xi = lax.fori_loop(0, n, body, (x0[:, :hs], x0[:, hs:]))
        xf_ref[d, 0, 0] = jnp.concatenate([xr, xi], axis=1)
        yacc = yacc + _dot(xp_scr[...].astype(BF16), wc_ref[d, 0])
    for j in range(cs):
        y_ref[pl.ds(j, n, stride=cs), :] = yacc[:, j * S5_HALF:(j + 1) * S5_HALF]


def _s5_segment(z, y_prev, x0, sp, seglen, seg_base, nb):
    r = z.shape[0]
    n = seglen // S5_CHUNK
    kw = S5_CHUNK * S5_HALF
    hs2 = 2 * S5_HSTATE
    kern = functools.partial(_s5_kernel, n=n)
    in_specs = [pl.BlockSpec((seglen, S5_HALF), lambda b, g: (seg_base + b, Z_S5 // S5_HALF + g)),
                pl.BlockSpec((1, kw, kw), lambda b, g: (g, 0, 0)),
                pl.BlockSpec((2, 1, kw, hs2), lambda b, g: (0, g, 0, 0)),
                pl.BlockSpec((2, 1, hs2, kw), lambda b, g: (0, g, 0, 0)),
                pl.BlockSpec((2, 1, 1, hs2), lambda b, g: (0, g, 0, 0)),
                pl.BlockSpec((2, 1, 1, 1, hs2), lambda b, g: (0, b, g, 0, 0)),
                pl.BlockSpec(memory_space=pl.ANY)]
    return pl.pallas_call(
        kern,
        grid=(nb, 2),
        in_specs=in_specs,
        out_specs=[pl.BlockSpec((seglen, S5_HALF), lambda b, g: (seg_base + b, g)),
                   pl.BlockSpec((2, 1, 1, 1, hs2), lambda b, g: (0, b, g, 0, 0))],
        out_shape=[jax.ShapeDtypeStruct((r, S5_WIDTH), F32), jax.ShapeDtypeStruct(x0.shape, F32)],
        scratch_shapes=[pltpu.VMEM((n, kw), BF16), pltpu.VMEM((n, hs2), F32), pltpu.VMEM((n, hs2), F32)],
        input_output_aliases={6: 0},
        compiler_params=_params("parallel", "parallel"),
        name="s5_segment",
    )(z, sp['ktoep'], sp['wb'], sp['wc'], sp['a_chunk'], x0, y_prev)


def _s5_operators(p):
    f32 = F32
    cs = S5_CHUNK
    g, n, hch = S5_GROUPS, S5_STATE, S5_GROUP_CH
    kts, wbs, wcs, acs = 0.0, [], [], []
    for d in range(2):
        lam = lax.complex(jnp.minimum(p['s5_a_re'][d].astype(f32), -1e-4), p['s5_a_im'][d].astype(f32))
        a_bar = jnp.exp(lam * jnp.exp(p['s5_log_dt'][d].astype(f32))[:, None])
        b_bar = ((a_bar - 1.0) / lam)[..., None] * lax.complex(p['s5_b_re'][d].astype(f32),
                                                               p['s5_b_im'][d].astype(f32))
        cc = lax.complex(p['s5_c_re'][d].astype(f32), p['s5_c_im'][d].astype(f32))
        pw = a_bar[None] ** jnp.arange(cs + 1, dtype=f32)[:, None, None].astype(jnp.complex64)
        ktau = jnp.einsum('ghn,tgn,gni->tghi', cc, pw[:cs], b_bar).real
        ii = jnp.arange(cs)[:, None]
        jj = jnp.arange(cs)[None, :]
        lag = (ii - jj) if d == 1 else (jj - ii)
        kt = jnp.where((lag >= 0)[:, :, None, None, None], ktau[jnp.clip(lag, 0, cs - 1)], 0.0)
        kts = kts + kt
        e_in = (jnp.arange(cs) if d == 1 else (cs - 1 - jnp.arange(cs)))
        wbc = pw[e_in][:, :, :, None] * b_bar[None]
        f_out = ((cs - jnp.arange(cs)) if d == 1 else (jnp.arange(cs) + 1))
        wcc = cc[None] * pw[f_out][:, :, None, :]
        wb_h, wc_h, ac_h = [], [], []
        for half in range(2):
            gs = slice(half * 8, half * 8 + 8)
            eye8 = jnp.eye(8, dtype=f32)
            wbg = wbc[:, gs]
            wre = jnp.einsum('ignh,gk->ighkn', wbg.real, eye8).reshape(cs * S5_HALF, S5_HSTATE)
            wim = jnp.einsum('ignh,gk->ighkn', wbg.imag, eye8).reshape(cs * S5_HALF, S5_HSTATE)
            wb_h.append(jnp.concatenate([wre, wim], axis=1))
            wcg = wcc[:, gs]
            cre = jnp.einsum('jghn,gk->knjgh', wcg.real, eye8).reshape(S5_HSTATE, cs * S5_HALF)
            cim = jnp.einsum('jghn,gk->knjgh', -wcg.imag, eye8).reshape(S5_HSTATE, cs * S5_HALF)
            wc_h.append(jnp.concatenate([cre, cim], axis=0))
            ach = pw[cs][gs].reshape(1, S5_HSTATE)
            ac_h.append(jnp.concatenate([ach.real, ach.imag], axis=1))
        wbs.append(jnp.stack(wb_h))
        wcs.append(jnp.stack(wc_h))
        acs.append(jnp.stack(ac_h))
    dsk = p['s5_d'].astype(f32).reshape(g, hch)
    eye_t = jnp.eye(cs, dtype=f32)[:, :, None, None, None]
    kts = kts + eye_t * (dsk[:, :, None] * jnp.eye(hch, dtype=f32)[None])[None, None]
    kt_h = []
    for half in range(2):
        gs = slice(half * 8, half * 8 + 8)
        eye8 = jnp.eye(8, dtype=f32)
        kt_h.append(jnp.einsum('ijgoh,gk->ighjko', kts[:, :, gs], eye8).reshape(cs * S5_HALF, cs * S5_HALF))
    return dict(ktoep=jnp.stack(kt_h).astype(BF16), wb=jnp.stack(wbs).astype(BF16),
                wc=jnp.stack(wcs).astype(BF16), a_chunk=jnp.stack(acs))


def _s5_branch(z, sp, dims):
    nb, seq, ctx = dims
    r = z.shape[0]
    x0 = jnp.zeros((2, nb, 2, 1, 2 * S5_HSTATE), F32)
    y0 = jnp.zeros((r, S5_WIDTH), F32)
    y_ctx, x_ctx = _s5_segment(z, y0, x0, sp, ctx, (nb * seq) // ctx, nb)
    y, _ = _s5_segment(z, y_ctx, x_ctx, sp, seq, 0, nb)
    return y


def _merge_kernel(x_ref, m_ref, yf_ref, yr_ref, bonus_ref, gdn_ref, hf_ref, hr_ref, o_ref, s5_ref, br_ref,
                  gup_ref, gng_ref, gnb_ref, bones64_ref, glw_ref, glb_ref, mlg_ref, bones96_ref, brb_ref,
                  uprw_ref, ups5_ref, upml_ref, wout_ref, lng_ref, lnb_ref, out_ref, *, nlat, seq, nb, tm):
    d = D_MODEL
    mrow = _mod_row(pl.program_id(0) * tm, nlat, seq, nb)
    gate = m_ref[pl.ds(mrow, 1), 5 * d:6 * d]
    ys = yf_ref[...] + yr_ref[...]
    b64 = bones64_ref[...]
    yc = ys - _mm_xr(ys, b64) * (1.0 / RW_HEAD_DIM)
    var = _mm_xr(yc * yc, b64) * (1.0 / RW_HEAD_DIM)
    yn = yc * lax.rsqrt(var + RW_GN_EPS) * gng_ref[...] + gnb_ref[...]
    rw_y = (yn + bonus_ref[...]) * _mm(jax.nn.sigmoid(gdn_ref[...]), gup_ref[...])
    s5 = jax.nn.gelu(s5_ref[...])
    s5_y = s5 * jax.nn.sigmoid(_mm(s5, glw_ref[...]) + glb_ref[...])
    hg = jax.nn.sigmoid(o_ref[...]) * (hf_ref[...] + hr_ref[...])
    b96 = bones96_ref[...]
    hc = hg - _mm_xr(hg, b96) * (1.0 / ML_HEAD_DIM)
    hv = _mm_xr(hc * hc, b96) * (1.0 / ML_HEAD_DIM)
    ml_y = hc * lax.rsqrt(hv + ML_EPS) * mlg_ref[...]
    bg = jax.nn.sigmoid(br_ref[...] + brb_ref[...])
    y = (bg[:, 0:d] * _mm(rw_y, uprw_ref[...]) + bg[:, d:2 * d] * _mm(s5_y, ups5_ref[...])
         + bg[:, 2 * d:3 * d] * _mm(ml_y, upml_ref[...]))
    delta = gate * _mm(y, wout_ref[...])
    out_ref[...] = _layer_norm(DEEPNORM_ALPHA * x_ref[...] + delta, lng_ref[...], lnb_ref[...])


def _merge(xs, mods, yf, yr, bonus, hf, hr, s5y, z, p, dims):
    nb, seq, ctx = dims
    r, d = xs.shape
    nlat = nb * seq
    tm = _pick_tile((512, 256, 128, 64), nlat, nb * ctx)
    w = RW_WIDTH
    row = lambda width, col=0: pl.BlockSpec((tm, width), lambda i: (i, col))
    full = lambda a: pl.BlockSpec(a.shape, lambda i: (0,) * a.ndim)
    consts = [p['g_up'], p['gn_g'], p['gn_b'], p['bones64'], p['glu_w'], p['glu_b'], p['ml_norm_g'], p['bones96'],
              p['br_b'], p['up_rw'], p['up_s5'], p['up_ml'], p['w_out'], p['ln_g1'], p['ln_b1']]
    kern = functools.partial(_merge_kernel, nlat=nlat, seq=seq, nb=nb, tm=tm)
    return pl.pallas_call(
        kern,
        grid=(r // tm,),
        in_specs=[row(d), pl.BlockSpec(mods.shape, lambda i: (0, 0)),
                  row(w), row(w), row(w), row(RW_GATE_LORA, Z_GDN // RW_GATE_LORA),
                  row(w), row(w), row(w, Z_MLO // w), row(S5_WIDTH), row(3 * d, 0)] + [full(a) for a in consts],
        out_specs=row(d),
        out_shape=jax.ShapeDtypeStruct((r, d), F32),
        compiler_params=_params("parallel"),
        name="merge",
    )(xs, mods, yf, yr, bonus, z, hf, hr, z, s5y, z, *consts)


def _permute_w_in(w):
    old = {}
    start = 0
    for name, size in (('conv', CONV_COLS), ('ml_v', ML_WIDTH), ('ml_o', ML_WIDTH), ('ml_gl', 4 * ML_HEADS),
                       ('s5_u', S5_WIDTH), ('w_dn', RW_DECAY_LORA), ('a_dn', RW_AAA_LORA), ('g_dn', RW_GATE_LORA),
                       ('br_gl', N_BRANCH * D_MODEL)):
        old[name] = w[:, start:start + size]
        start += size
    pad = jnp.zeros((w.shape[0], LANES - 4 * ML_HEADS), w.dtype)
    out = jnp.concatenate([old['br_gl'], old['conv'], old['ml_v'], old['ml_o'], old['w_dn'], old['a_dn'],
                           old['s5_u'], old['g_dn'], old['ml_gl'], pad], axis=1)
    assert out.shape[1] == Z_WIDTH
    return out


def _block_ones(width, group):
    g = np.arange(width) // group
    return (g[:, None] == g[None, :]).astype(np.float32)


def _layer_params(i, a):
    w = RW_WIDTH
    pad_lo = jnp.zeros((2, RW_AAA_LORA, w), F32)
    gb = a['ml_gate_b'][i].reshape(16).astype(F32)
    return dict(
        k_k=a['rw_k_k'][i].reshape(1, w), k_a=a['rw_k_a'][i].reshape(1, w), r_k=a['rw_r_k'][i].reshape(1, w),
        w0=a['rw_w0'][i].reshape(2, 1, w), a0=a['rw_a0'][i].reshape(2, 1, w),
        w_up=jnp.concatenate([a['rw_w_up'][i], pad_lo], axis=1).astype(BF16),
        a_up=jnp.concatenate([pad_lo, a['rw_a_up'][i]], axis=1).astype(BF16),
        bones64=jnp.asarray(_block_ones(w, RW_HEAD_DIM), BF16),
        bones96=jnp.asarray(_block_ones(ML_WIDTH, ML_HEAD_DIM), BF16),
        bmask96=jnp.asarray(_block_ones(ML_WIDTH, ML_HEAD_DIM), F32),
        gb_row=jnp.concatenate([gb, jnp.zeros((LANES - 16,), F32)]).reshape(1, LANES), gb_col=gb.reshape(16, 1),
        g_up=a['rw_g_up'][i].astype(BF16), gn_g=a['rw_gn_g'][i].reshape(1, w), gn_b=a['rw_gn_b'][i].reshape(1, w),
        glu_w=a['s5_glu_w'][i].astype(BF16), glu_b=a['s5_glu_b'][i].reshape(1, S5_WIDTH),
        ml_norm_g=a['ml_norm_g'][i].reshape(1, ML_WIDTH), br_b=a['br_gate_b'][i].reshape(1, N_BRANCH * D_MODEL),
        up_rw=a['up_rw'][i].astype(BF16), up_s5=a['up_s5'][i].astype(BF16), up_ml=a['up_ml'][i].astype(BF16),
        w_out=a['w_out'][i].astype(BF16),
        ln_g1=a['ln_g'][i, 1].reshape(1, D_MODEL), ln_b1=a['ln_b'][i, 1].reshape(1, D_MODEL),
    )


def kernel(x, c, ctx, c_ctx, ada_w, ada_b, ln_g, ln_b, ffn_w_gate, ffn_w_up, ffn_w_down, w_in, conv_w, rw_w0, rw_w_up, rw_a0, rw_a_up, rw_g_up, rw_k_k, rw_k_a, rw_r_k, rw_gn_g, rw_gn_b, s5_a_re, s5_a_im, s5_log_dt, s5_b_re, s5_b_im, s5_c_re, s5_c_im, s5_d, s5_glu_w, s5_glu_b, ml_gate_b, ml_norm_g, up_rw, up_s5, up_ml, br_gate_b, w_out):
    a = dict(ln_g=ln_g, ln_b=ln_b, rw_w0=rw_w0, rw_w_up=rw_w_up, rw_a0=rw_a0, rw_a_up=rw_a_up, rw_g_up=rw_g_up,
             rw_k_k=rw_k_k, rw_k_a=rw_k_a, rw_r_k=rw_r_k, rw_gn_g=rw_gn_g, rw_gn_b=rw_gn_b, s5_glu_w=s5_glu_w,
             s5_glu_b=s5_glu_b, ml_gate_b=ml_gate_b, ml_norm_g=ml_norm_g, up_rw=up_rw, up_s5=up_s5, up_ml=up_ml,
             br_gate_b=br_gate_b, w_out=w_out)
    nb, seq, d = x.shape
    nctx = ctx.shape[1]
    dims = (nb, seq, nctx)
    depth = ada_w.shape[0]
    rows = -(-(nb + 1) // SUBLANES) * SUBLANES
    cv = jnp.concatenate([c, c_ctx[None, :], jnp.zeros((rows - nb - 1, d), F32)], axis=0)
    mods = _modulation(cv, ada_w, ada_b)
    xs = jnp.concatenate([x.reshape(nb * seq, d), ctx.reshape(nb * nctx, d)], axis=0)
    for i in range(depth):
        m = mods[i]
        p = _layer_params(i, a)
        sp = _s5_operators(dict(s5_a_re=s5_a_re[i], s5_a_im=s5_a_im[i], s5_log_dt=s5_log_dt[i], s5_b_re=s5_b_re[i],
                                s5_b_im=s5_b_im[i], s5_c_re=s5_c_re[i], s5_c_im=s5_c_im[i], s5_d=s5_d[i]))
        w_in_p = _permute_w_in(w_in[i].astype(BF16))
        xs = _ffn_half(xs, m, 0, ffn_w_gate[i, 0].astype(BF16), ffn_w_up[i, 0].astype(BF16),
                       ffn_w_down[i, 0].astype(BF16), ln_g[i, 0], ln_b[i, 0], dims)
        z = _in_proj(xs, m, w_in_p, dims)
        zc = _short_conv(z, conv_w[i], dims)
        y1, rq, pt, qt, bonus = _rwkv_chunk(zc, z, p, dims)
        yf, yr = _rwkv_scan(y1, rq, pt, qt, dims)
        glt = z[:, Z_MLG:Z_MLG + 16].reshape(-1, CHUNK, 16).transpose(0, 2, 1)
        hf, hr = _mlstm(zc, z, glt, p, dims)
        s5y = _s5_branch(z, sp, dims)
        xs = _merge(xs, m, yf, yr, bonus, hf, hr, s5y, z, p, dims)
        xs = _ffn_half(xs, m, 6, ffn_w_gate[i, 1].astype(BF16), ffn_w_up[i, 1].astype(BF16),
                       ffn_w_down[i, 1].astype(BF16), ln_g[i, 2], ln_b[i, 2], dims)
    return xs[:nb * seq].reshape(nb, seq, d)
```

```python
import functools

import numpy as np
import jax
import jax.numpy as jnp
from jax import lax
from jax.experimental import pallas as pl
from jax.experimental.pallas import tpu as pltpu

F32 = jnp.float32
BF16 = jnp.bfloat16

D_MODEL = 1024
DEPTH = 2
GRID_W = 64

RW_HEADS = 6
RW_HEAD_DIM = 64
RW_WIDTH = 384
RW_DECAY_LORA = 64
RW_AAA_LORA = 64
RW_GATE_LORA = 128
RW_GN_EPS = 64e-5

S5_GROUPS = 16
S5_GROUP_CH = 16
S5_WIDTH = 256
S5_STATE = 64

ML_HEADS = 4
ML_HEAD_DIM = 96
ML_WIDTH = 384
ML_EPS = 1e-5

D_FF = 2816
N_BRANCH = 3
N_MOD = 9
LN_EPS = 1e-5
DEEPNORM_ALPHA = (2.0 * DEPTH) ** 0.25

LANES = 128
SUBLANES = 8
MXU_COLS = 256

CHUNK = 64
S5_CHUNK = 8
S5_HALF = 128
S5_HSTATE = 512
ML_PAD = LANES
ML_PW = ML_HEADS * ML_PAD
RW_OPS = 4 * RW_WIDTH

Z_RWC = 0
Z_LORA = Z_RWC + 3 * RW_WIDTH
Z_GDN = Z_LORA + LANES
Z_MLG = Z_GDN + RW_GATE_LORA
Z_S5 = Z_MLG + LANES
Z_MLV = 2048
Z_MLO = Z_MLV + ML_PW
Z_MLC = Z_MLO + ML_PW
Z_BR = Z_MLC + 2 * ML_PW
Z_WIDTH = Z_BR + N_BRANCH * D_MODEL

VMEM_LIMIT = 58 * 1024 * 1024

_NN = (((1,), (0,)), ((), ()))
_NT = (((1,), (1,)), ((), ()))
_TN = (((0,), (0,)), ((), ()))


def _dot(a, b, dims=_NN):
    return lax.dot_general(a, b, dims, preferred_element_type=F32)


def _mm(a, b, dims=_NN):
    return _dot(a.astype(BF16), b.astype(BF16), dims)


def _split(a, n):
    parts, r = [], a
    for _ in range(n):
        p = r.astype(BF16)
        parts.append(p)
        r = r - p.astype(F32)
    return parts


def _mm_xr(a, b_exact, dims=_NN):
    ps = _split(a, 3)
    return _dot(ps[0], b_exact, dims) + (_dot(ps[1], b_exact, dims) + _dot(ps[2], b_exact, dims))


def _mm_xl(a_exact, b, dims=_NN):
    ps = _split(b, 3)
    return _dot(a_exact, ps[0], dims) + (_dot(a_exact, ps[1], dims) + _dot(a_exact, ps[2], dims))


def _mm3(a, b, dims=_NN):
    ah, al = _split(a, 2)
    bh, bl = _split(b, 2)
    return _dot(ah, bh, dims) + (_dot(al, bh, dims) + _dot(ah, bl, dims))


def _softplus(x):
    return jnp.maximum(x, 0.0) + jnp.log1p(jnp.exp(-jnp.abs(x)))


def _layer_norm(y, g, b):
    yc = y - jnp.mean(y, axis=-1, keepdims=True)
    return yc * lax.rsqrt(jnp.mean(yc * yc, axis=-1, keepdims=True) + LN_EPS) * g + b


def _pick_tile(cands, *lens):
    for t in cands:
        if all(n % t == 0 for n in lens):
            return t
    raise ValueError(f"no tile in {cands} divides {lens}")


def _params(*sem):
    return pltpu.CompilerParams(dimension_semantics=sem, vmem_limit_bytes=VMEM_LIMIT)


def _const(shape):
    zeros = (0,) * len(shape)
    return pl.BlockSpec(shape, lambda *_: zeros, pipeline_mode=pl.Buffered(1))


def _mod_vec(m_ref, is_ctx, b, nb, k):
    row = jnp.where(is_ctx, nb, b)
    return m_ref[pl.ds(row, 1), k * D_MODEL:(k + 1) * D_MODEL]


def _mod_kernel(cv_ref, w_ref, b_ref, o_ref):
    s = cv_ref[...]
    s = s * jax.nn.sigmoid(s)
    o_ref[0] = _mm3(s, w_ref[0]) + b_ref[0]


def _modulation(cv, ada_w, ada_b):
    depth, d, nd = ada_w.shape
    rows = cv.shape[0]
    return pl.pallas_call(
        _mod_kernel,
        grid=(depth, nd // d),
        in_specs=[pl.BlockSpec((rows, d), lambda l, j: (0, 0)),
                  pl.BlockSpec((1, d, d), lambda l, j: (l, 0, j)),
                  pl.BlockSpec((1, 1, d), lambda l, j: (l, 0, j))],
        out_specs=pl.BlockSpec((1, rows, d), lambda l, j: (l, 0, j)),
        out_shape=jax.ShapeDtypeStruct((depth, rows, nd), F32),
        compiler_params=_params("parallel", "parallel"),
        name="modulation",
    )(cv, ada_w, ada_b.reshape(depth, 1, nd))


def _ffn_kernel(x_ref, m_ref, wg_ref, wu_ref, wd_ref, g_ref, b_ref, *rest, k0, nb, tr, n_ctx_tiles):
    o_ref, u_scr, h_scr = rest[-3:]
    is_ctx = pl.program_id(0) < n_ctx_tiles
    for b in range(nb):
        shift = _mod_vec(m_ref, is_ctx, b, nb, k0)
        scale = _mod_vec(m_ref, is_ctx, b, nb, k0 + 1)
        u_scr[b * tr:(b + 1) * tr, :] = (x_ref[b] * (1.0 + scale) + shift).astype(BF16)
    u = u_scr[...]
    for f in range(D_FF // MXU_COLS):
        cols = slice(f * MXU_COLS, (f + 1) * MXU_COLS)
        g = _dot(u, wg_ref[:, cols])
        up = _dot(u, wu_ref[:, cols])
        h_scr[:, cols] = (g * jax.nn.sigmoid(g) * up).astype(BF16)
    acc = _dot(h_scr[...], wd_ref[...])
    for b in range(nb):
        gate = _mod_vec(m_ref, is_ctx, b, nb, k0 + 2)
        y = DEEPNORM_ALPHA * x_ref[b] + (0.5 * gate) * acc[b * tr:(b + 1) * tr, :]
        o_ref[b] = _layer_norm(y, g_ref[...], b_ref[...])


def _ffn_half(src, mods, k0, wg, wu, wd, g, b, dims, *, src_tile0, n_tiles, n_ctx_tiles, dst_rows, dst_tile0,
              dst=None):
    nb, seq, ctx = dims
    d = D_MODEL
    tr = _ffn_tile(dims)
    kern = functools.partial(_ffn_kernel, k0=k0, nb=nb, tr=tr, n_ctx_tiles=n_ctx_tiles)
    in_specs = [pl.BlockSpec((nb, tr, d), lambda i: (0, i + src_tile0, 0)),
                _const(mods.shape), _const((d, D_FF)), _const((d, D_FF)), _const((D_FF, d)),
                _const((1, d)), _const((1, d))]
    args = [src, mods, wg, wu, wd, g.reshape(1, d), b.reshape(1, d)]
    aliases = {}
    if dst is not None:
        in_specs.append(pl.BlockSpec(memory_space=pl.ANY))
        args.append(dst)
        aliases = {len(args) - 1: 0}
    return pl.pallas_call(
        kern,
        grid=(n_tiles,),
        in_specs=in_specs,
        out_specs=pl.BlockSpec((nb, tr, d), lambda i: (0, i + dst_tile0, 0)),
        out_shape=jax.ShapeDtypeStruct((nb, dst_rows, d), F32),
        scratch_shapes=[pltpu.VMEM((nb * tr, d), BF16), pltpu.VMEM((nb * tr, D_FF), BF16)],
        input_output_aliases=aliases,
        compiler_params=_params("parallel"),
        name="ffn_half",
    )(*args)


def _ffn_tile(dims):
    return _pick_tile((256, 128, 64), dims[1], dims[2])


def _inproj_kernel(x_ref, m_ref, w_ref, o_ref, u_scr, *, nb, tr, ctx_tiles):
    is_ctx = pl.program_id(0) < ctx_tiles

    @pl.when(pl.program_id(1) == 0)
    def _():
        for b in range(nb):
            shift = _mod_vec(m_ref, is_ctx, b, nb, 3)
            scale = _mod_vec(m_ref, is_ctx, b, nb, 4)
            u_scr[b * tr:(b + 1) * tr, :] = (x_ref[b] * (1.0 + scale) + shift).astype(BF16)

    res = _dot(u_scr[...], w_ref[...])
    for b in range(nb):
        o_ref[b] = res[b * tr:(b + 1) * tr, :]


def _in_proj(xs, mods, w_in_p, dims):
    nb, seq, ctx = dims
    d = D_MODEL
    lt = seq + ctx
    tr = _pick_tile((256, 128, 64), seq, ctx)
    tn = 7 * MXU_COLS
    kern = functools.partial(_inproj_kernel, nb=nb, tr=tr, ctx_tiles=ctx // tr)
    return pl.pallas_call(
        kern,
        grid=(lt // tr, Z_WIDTH // tn),
        in_specs=[pl.BlockSpec((nb, tr, d), lambda i, j: (0, i, 0)),
                  pl.BlockSpec(mods.shape, lambda i, j: (0, 0)),
                  pl.BlockSpec((d, tn), lambda i, j: (0, j))],
        out_specs=pl.BlockSpec((nb, tr, tn), lambda i, j: (0, i, j)),
        out_shape=jax.ShapeDtypeStruct((nb, lt, Z_WIDTH), F32),
        scratch_shapes=[pltpu.VMEM((nb * tr, d), BF16)],
        compiler_params=_params("parallel", "arbitrary"),
        name="in_proj",
    )(xs, mods, w_in_p)


def _conv_kernel(prev_ref, cur_ref, next_ref, w_ref, o_ref, *, tmc, ctx, lt):
    row0 = pl.program_id(1) * tmc
    is_lat = row0 >= ctx
    has_prev = row0 > ctx
    has_next = jnp.logical_and(is_lat, row0 + tmc < lt)
    pos = lax.broadcasted_iota(jnp.int32, (tmc, 1), 0)
    col = lax.rem(pos, GRID_W)
    keep = lambda drop: jnp.where(drop, 0.0, 1.0)
    m_left = jnp.where(is_lat, keep(col == 0), keep(jnp.logical_and(pos == 0, row0 == 0)))
    m_right = jnp.where(is_lat, keep(col == GRID_W - 1),
                        keep(jnp.logical_and(pos == tmc - 1, row0 + tmc == ctx)))
    cur = cur_ref[0]
    prev = jnp.where(has_prev, prev_ref[0], 0.0)
    nxt = jnp.where(has_next, next_ref[0], 0.0)
    if tmc > GRID_W:
        up = jnp.concatenate([prev, cur[:tmc - GRID_W]], axis=0)
        dn = jnp.concatenate([cur[GRID_W:], nxt], axis=0)
    else:
        up, dn = prev, nxt
    lat_f = jnp.where(is_lat, 1.0, 0.0)
    w = w_ref[...]
    w_up = w[0] * lat_f
    w_dn = w[2] * lat_f

    def tap(j):
        return up * w_up[j:j + 1, :] + cur * w[1, j:j + 1, :] + dn * w_dn[j:j + 1, :]

    o_ref[0] = m_left * pltpu.roll(tap(0), 1, 0) + tap(1) + m_right * pltpu.roll(tap(2), tmc - 1, 0)


def _short_conv(z, conv_w, col0, dims):
    nb, seq, ctx = dims
    lt = seq + ctx
    cw = conv_w.shape[-1]
    tmc = _pick_tile((256, 128, 64), seq, ctx)
    assert tmc == ctx, "the context sequence conv expects one tile per context segment"
    hb = tmc // GRID_W
    nblk = lt // GRID_W
    cb = col0 // cw
    kern = functools.partial(_conv_kernel, tmc=tmc, ctx=ctx, lt=lt)
    return pl.pallas_call(
        kern,
        grid=(nb, lt // tmc),
        in_specs=[pl.BlockSpec((1, GRID_W, cw), lambda b, i: (b, jnp.maximum(i * hb - 1, 0), cb)),
                  pl.BlockSpec((1, tmc, cw), lambda b, i: (b, i, cb)),
                  pl.BlockSpec((1, GRID_W, cw), lambda b, i: (b, jnp.minimum(i * hb + hb, nblk - 1), cb)),
                  pl.BlockSpec((3, 3, cw), lambda b, i: (0, 0, 0))],
        out_specs=pl.BlockSpec((1, tmc, cw), lambda b, i: (b, i, 0)),
        out_shape=jax.ShapeDtypeStruct((nb, lt, cw), F32),
        compiler_params=_params("parallel", "parallel"),
        name="short_conv",
    )(z, z, z, conv_w)


def _stream_chunk(s, rev, nc, nl):
    if not rev:
        return s
    return jnp.where(s < nc, nc - 1 - s, 2 * nc + nl - 1 - s)


def _pair_masks():
    lane = lax.broadcasted_iota(jnp.int32, (1, LANES), 1)
    m0 = (lane < RW_HEAD_DIM).astype(F32)
    return m0, 1.0 - m0


def _make_bd(m0, m1):
    m0h, m1h = m0.astype(BF16), m1.astype(BF16)

    def bd(y):
        y = y.astype(BF16)
        n = y.shape[1] // LANES
        a0 = jnp.concatenate([m0h] * n, axis=1) if n > 1 else m0h
        a1 = jnp.concatenate([m1h] * n, axis=1) if n > 1 else m1h
        return jnp.concatenate([y * a0, y * a1], axis=0)

    return bd


def _rwkv_chunk_kernel(r_ref, k_ref, v_ref, lo_ref, kk_ref, ka_ref, rk_ref, w0_ref, wup_ref, a0_ref, aup_ref,
                       bones_ref, ops_ref, bonus_ref, *, nch):
    c = CHUNK
    w = RW_WIDTH
    bones = bones_ref[...]
    m0, m1 = _pair_masks()
    bd = _make_bd(m0, m1)

    ti = lax.broadcasted_iota(jnp.int32, (c, c), 0)
    tj = lax.broadcasted_iota(jnp.int32, (c, c), 1)
    pt_i = lax.broadcasted_iota(jnp.int32, (c, LANES), 0)
    pt_j = lax.rem(lax.broadcasted_iota(jnp.int32, (c, LANES), 1), c)
    eye_p = (pt_i == pt_j).astype(F32)
    masks = []
    for d in range(2):
        strict = ((pt_j > pt_i) if d == 1 else (pt_j < pt_i)).astype(F32)
        masks.append((((tj >= ti) if d == 1 else (tj <= ti)).astype(BF16), strict, strict + eye_p))

    chains = []
    for q in range(nch):
        rows = slice(q * c, (q + 1) * c)
        r = r_ref[0, rows, :]
        k = k_ref[0, rows, :]
        v = v_ref[0, rows, :]
        lo = lo_ref[0, rows, :]
        kk = k * kk_ref[...]
        kk = kk * lax.rsqrt(jnp.maximum(_mm_xr(kk * kk, bones), 1e-24))
        tw = jnp.tanh(lo)
        bonus = jnp.zeros_like(r)
        for d in range(2):
            rev = d == 1
            tri, strict, incl = masks[d]
            wl = w0_ref[d] + _mm(tw, wup_ref[d])
            lw = -jnp.exp(-_softplus(-wl) - 0.5)
            a = jax.nn.sigmoid(a0_ref[d] + _mm(lo, aup_ref[d]))
            kd = k * (1.0 + (a - 1.0) * ka_ref[...])
            bb = kk * a
            bonus = bonus + _mm_xr(r * kd * rk_ref[...], bones) * v
            cs = _mm_xl(tri, lw)
            csp = cs - lw
            tot = cs[0:1, :] if rev else cs[c - 1:c, :]
            mid = 0.5 * tot
            e_dn = jnp.exp(mid - cs)
            a_hat = kk * jnp.exp(csp - mid)
            r_hat = r * jnp.exp(cs - mid)
            b_hat = bb * e_dn
            k_hat = kd * e_dn
            a_st = kk * jnp.exp(csp)
            r_st = r * jnp.exp(cs)
            e_end = jnp.exp(tot - cs)
            b_end = bb * e_end
            k_end = kd * e_end
            w_c = jnp.exp(tot)
            for p in range(RW_HEADS // 2):
                sl = slice(p * LANES, (p + 1) * LANES)
                chains.append(dict(d=d, rows=rows, p=p, strict=strict, incl=incl, vp=v[:, sl], a_hat=a_hat[:, sl],
                                   r_hat=r_hat[:, sl], b_hat=b_hat[:, sl], k_hat=k_hat[:, sl], a_st=a_st[:, sl],
                                   r_st=r_st[:, sl], b_end=b_end[:, sl], k_end=k_end[:, sl], w_c=w_c[:, sl]))
        bonus_ref[0, rows, :] = bonus

    for ch in chains:
        lhs = jnp.concatenate([ch['a_hat'], ch['r_hat']], axis=0).astype(BF16)
        rhs_t = jnp.concatenate([bd(ch['b_hat']), bd(ch['k_hat'])], axis=0)
        g4 = _dot(lhs, rhs_t, _NT)
        ch['lp'] = g4[0:c, 0:LANES] * ch['strict']
        ch['t'] = eye_p - ch['lp']
        ch['m_rb'] = (g4[c:, 0:LANES] * ch['incl']).astype(BF16)
        ch['lak_mrk'] = jnp.concatenate([g4[0:c, LANES:] * ch['strict'], g4[c:, LANES:] * ch['incl']],
                                        axis=0).astype(BF16)
    for ch in chains:
        ch['lp'] = _dot(ch['lp'].astype(BF16), bd(ch['lp']))
    for it in range(5):
        for ch in chains:
            if it < 4:
                res = _dot(jnp.concatenate([ch['t'], ch['lp']], axis=0).astype(BF16), bd(ch['lp']))
                ch['t'] = ch['t'] + res[0:c]
                ch['lp'] = res[c:]
            else:
                ch['t'] = ch['t'] + _dot(ch['t'].astype(BF16), bd(ch['lp']))
    for ch in chains:
        xv = _dot(ch['lak_mrk'], bd(ch['vp']))
        ch['x0'] = xv[0:c]
        ch['mrk_v'] = xv[c:]
    for ch in chains:
        ch['ua'] = _dot(ch['t'].astype(BF16), bd(jnp.concatenate([ch['x0'], ch['a_st']], axis=1)))
    for ch in chains:
        d, rows, p = ch['d'], ch['rows'], ch['p']
        mu = _dot(ch['m_rb'], bd(ch['ua']))
        ops_ref[d, 0, rows, p * LANES:(p + 1) * LANES] = ch['mrk_v'] - mu[:, 0:LANES]
        ops_ref[d, 0, rows, w + p * LANES:w + (p + 1) * LANES] = ch['r_st'] - mu[:, LANES:]
    for ch in chains:
        d, rows, p = ch['d'], ch['rows'], ch['p']
        ua = ch['ua']
        rhs = jnp.concatenate([jnp.concatenate([ch['vp'], jnp.zeros_like(ch['vp'])], axis=1),
                               jnp.concatenate([-ua[:, 0:LANES], ua[:, LANES:]], axis=1)], axis=0)
        hg = _mm(jnp.concatenate([ch['k_end'], ch['b_end']], axis=0), rhs, _TN)
        ops_ref[d, 0, rows, 2 * w + p * LANES:2 * w + (p + 1) * LANES] = (
            eye_p * ch['w_c'] - (hg[0:c, LANES:] * m0 + hg[c:, LANES:] * m1))
        ops_ref[d, 0, rows, 3 * w + p * LANES:3 * w + (p + 1) * LANES] = hg[0:c, 0:LANES] * m0 + hg[c:, 0:LANES] * m1


def _rwkv_chunk(zc, z, p, dims):
    nb, seq, ctx = dims
    lt = seq + ctx
    w = RW_WIDTH
    nch = 2 if (seq // CHUNK) % 2 == 0 and (ctx // CHUNK) % 2 == 0 else 1
    c = nch * CHUNK
    full = _const
    return pl.pallas_call(
        functools.partial(_rwkv_chunk_kernel, nch=nch),
        grid=(nb, lt // c),
        in_specs=[pl.BlockSpec((1, c, w), lambda b, i: (b, i, 0)),
                  pl.BlockSpec((1, c, w), lambda b, i: (b, i, 1)),
                  pl.BlockSpec((1, c, w), lambda b, i: (b, i, 2)),
                  pl.BlockSpec((1, c, LANES), lambda b, i: (b, i, Z_LORA // LANES)),
                  full((1, w)), full((1, w)), full((1, w)),
                  full((2, 1, w)), full((2, LANES, w)), full((2, 1, w)), full((2, LANES, w)),
                  full((w, w))],
        out_specs=[pl.BlockSpec((2, 1, c, RW_OPS), lambda b, i: (0, b, i, 0)),
                   pl.BlockSpec((1, c, w), lambda b, i: (b, i, 0))],
        out_shape=[jax.ShapeDtypeStruct((2, nb, lt, RW_OPS), F32), jax.ShapeDtypeStruct((nb, lt, w), F32)],
        compiler_params=_params("parallel", "parallel"),
        name="rwkv_chunk",
    )(zc, zc, zc, z, p['k_k'], p['k_a'], p['r_k'], p['w0'], p['w_up'], p['a0'], p['a_up'], p['bones64'])


def _rwkv_scan_kernel(opf_ref, opr_ref, yf_ref, yr_ref, st_scr, *, nb):
    c = CHUNK
    w = RW_WIDTH

    @pl.when(pl.program_id(0) == 0)
    def _():
        st_scr[...] = jnp.zeros_like(st_scr)

    m0, m1 = _pair_masks()
    bd = _make_bd(m0, m1)
    work = []
    for d, (op, yo) in enumerate(((opf_ref, yf_ref), (opr_ref, yr_ref))):
        for b in range(nb):
            for p in range(RW_HEADS // 2):
                work.append((d, b, p, op, yo))
    res = []
    for d, b, p, op, yo in work:
        sl = slice(p * LANES, (p + 1) * LANES)
        lhs = jnp.concatenate([op[0, b, :, w + p * LANES:w + (p + 1) * LANES],
                               op[0, b, :, 2 * w + p * LANES:2 * w + (p + 1) * LANES]], axis=0).astype(BF16)
        res.append(_dot(lhs, bd(st_scr[d, b, :, sl])))
    for (d, b, p, op, yo), rs in zip(work, res):
        sl = slice(p * LANES, (p + 1) * LANES)
        yo[b, :, sl] = op[0, b, :, sl] + rs[0:c]
        st_scr[d, b, :, sl] = rs[c:] + op[0, b, :, 3 * w + p * LANES:3 * w + (p + 1) * LANES]


def _rwkv_scan(ops, dims):
    nb, seq, ctx = dims
    lt = seq + ctx
    w = RW_WIDTH
    c = CHUNK
    nc, nl = ctx // c, seq // c
    out = jax.ShapeDtypeStruct((nb, lt, w), F32)
    return pl.pallas_call(
        functools.partial(_rwkv_scan_kernel, nb=nb),
        grid=(nc + nl,),
        in_specs=[pl.BlockSpec((1, nb, c, RW_OPS), lambda s: (0, 0, _stream_chunk(s, False, nc, nl), 0)),
                  pl.BlockSpec((1, nb, c, RW_OPS), lambda s: (1, 0, _stream_chunk(s, True, nc, nl), 0))],
        out_specs=[pl.BlockSpec((nb, c, w), lambda s: (0, _stream_chunk(s, False, nc, nl), 0)),
                   pl.BlockSpec((nb, c, w), lambda s: (0, _stream_chunk(s, True, nc, nl), 0))],
        out_shape=[out, out],
        scratch_shapes=[pltpu.VMEM((2, nb, c, w), F32)],
        compiler_params=_params("arbitrary"),
        name="rwkv_scan",
    )(ops, ops)


def _mlstm_kernel(qf, kf, vf, gcf, gtf, qr, kr, vr, gcr, gtr, gbr_ref, gbc_ref, hf_ref, hr_ref,
                  c_scr, n_scr, m_scr, *, nb):
    c = CHUNK
    nh, dh, pw = ML_HEADS, ML_HEAD_DIM, ML_PAD

    @pl.when(pl.program_id(0) == 0)
    def _():
        c_scr[...] = jnp.zeros_like(c_scr)
        n_scr[...] = jnp.zeros_like(n_scr)
        m_scr[...] = jnp.zeros_like(m_scr)

    lane1 = lax.broadcasted_iota(jnp.int32, (1, LANES), 1)
    ti = lax.broadcasted_iota(jnp.int32, (c, c), 0)
    tj = lax.broadcasted_iota(jnp.int32, (c, c), 1)

    chains = []
    for d, (q_ref, k_ref, v_ref, gc_ref, gt_ref, h_ref) in enumerate(
            ((qf, kf, vf, gcf, gtf, hf_ref), (qr, kr, vr, gcr, gtr, hr_ref))):
        rev = d == 1
        valid = (tj >= ti) if rev else (tj <= ti)
        tri = valid.astype(BF16)
        for b in range(nb):
            q = q_ref[b]
            q = q * jax.nn.sigmoid(q)
            k = k_ref[b]
            k = k * jax.nn.sigmoid(k) * (dh ** -0.5)
            gc = gc_ref[b] + gbr_ref[...]
            gt = gt_ref[b, 0] + gbc_ref[...]
            fc = -_softplus(-gc)
            ft = -_softplus(-gt)
            chains.append(dict(d=d, b=b, rev=rev, valid=valid, q=q, k=k, v=v_ref[b], gc=gc, gt=gt,
                               bc_cols=_mm_xl(tri, fc),
                               bc_rows=_mm_xr(ft, tri, _NT), h_ref=h_ref, m_all=m_scr[d, b]))
    for ch in chains:
        d, rev = ch['d'], ch['rev']
        heads = []
        for h in range(nh):
            hs = slice(h * pw, (h + 1) * pw)
            ci, cf = d * 8 + h, d * 8 + 4 + h
            bcc = ch['bc_cols'][:, cf:cf + 1]
            icc = ch['gc'][:, ci:ci + 1]
            bcr = ch['bc_rows'][cf:cf + 1, :]
            icr = ch['gt'][ci:ci + 1, :]
            b_last = bcc[0:1, :] if rev else bcc[c - 1:c, :]
            m_prev = ch['m_all'][:, h:h + 1]
            logd = jnp.where(ch['valid'], bcc - bcr + icr, -jnp.inf)
            log_inter = bcc + m_prev
            m = jnp.maximum(log_inter, jnp.max(logd, axis=-1, keepdims=True))
            inter = jnp.exp(log_inter - m)
            sm = _mm(ch['q'][:, hs], ch['k'][:, hs], _NT) * jnp.exp(logd - m)
            log_w = b_last - bcc + icc
            m_new = jnp.maximum(b_last + m_prev, jnp.max(log_w, axis=0, keepdims=True))
            heads.append(dict(hs=hs, sm=sm, inter=inter, m=m, m_new=m_new, wgt=jnp.exp(log_w - m_new),
                              dec=jnp.exp(b_last + m_prev - m_new)))
        ch['heads'] = heads
    for ch in chains:
        d, b = ch['d'], ch['b']
        for h, hd in enumerate(ch['heads']):
            hs = hd['hs']
            qh = ch['q'][:, hs]
            c_prev = c_scr[d, b, h]
            n_prev = n_scr[d, b, :, hs]
            num = hd['inter'] * _mm(qh, c_prev) + _mm(hd['sm'], ch['v'][:, hs])
            den = hd['inter'] * jnp.sum(qh * n_prev, axis=-1, keepdims=True) + jnp.sum(hd['sm'], axis=-1,
                                                                                        keepdims=True)
            ch['h_ref'][b, :, hs] = num / jnp.maximum(jnp.abs(den), jnp.exp(-hd['m']))
    for ch in chains:
        d, b = ch['d'], ch['b']
        m_new_all = jnp.zeros((1, LANES), F32)
        for h, hd in enumerate(ch['heads']):
            hs = hd['hs']
            kw = ch['k'][:, hs] * hd['wgt']
            c_scr[d, b, h] = hd['dec'] * c_scr[d, b, h] + _mm(kw, ch['v'][:, hs], _TN)
            n_scr[d, b, :, hs] = hd['dec'] * n_scr[d, b, :, hs] + jnp.sum(kw, axis=0, keepdims=True)
            m_new_all = m_new_all + hd['m_new'] * (lane1 == h).astype(F32)
        m_scr[d, b] = m_new_all


def _mlstm(zc_ml, z, glt, p, dims):
    nb, seq, ctx = dims
    lt = seq + ctx
    c = CHUNK
    nc, nl = ctx // c, seq // c
    pw = ML_PW

    def specs(rev):
        ck = lambda s: _stream_chunk(s, rev, nc, nl)
        return [pl.BlockSpec((nb, c, pw), lambda s: (0, ck(s), 0)),
                pl.BlockSpec((nb, c, pw), lambda s: (0, ck(s), 1)),
                pl.BlockSpec((nb, c, pw), lambda s: (0, ck(s), Z_MLV // pw)),
                pl.BlockSpec((nb, c, LANES), lambda s: (0, ck(s), Z_MLG // LANES)),
                pl.BlockSpec((nb, 1, 16, c), lambda s: (0, ck(s), 0, 0))]

    out = jax.ShapeDtypeStruct((nb, lt, pw), F32)
    return pl.pallas_call(
        functools.partial(_mlstm_kernel, nb=nb),
        grid=(nc + nl,),
        in_specs=specs(False) + specs(True) + [_const((1, LANES)), _const((16, 1))],
        out_specs=[pl.BlockSpec((nb, c, pw), lambda s: (0, _stream_chunk(s, False, nc, nl), 0)),
                   pl.BlockSpec((nb, c, pw), lambda s: (0, _stream_chunk(s, True, nc, nl), 0))],
        out_shape=[out, out],
        scratch_shapes=[pltpu.VMEM((2, nb, ML_HEADS, ML_PAD, ML_PAD), F32), pltpu.VMEM((2, nb, 1, pw), F32),
                        pltpu.VMEM((2, nb, 1, LANES), F32)],
        compiler_params=_params("arbitrary"),
        name="mlstm",
    )(zc_ml, zc_ml, z, z, glt, zc_ml, zc_ml, z, z, glt, p['gb_row'], p['gb_col'])


def _s5_kernel(u_ref, kt_ref, wb_ref, wc_ref, a_ref, y_ref, lhs_scr, st_scr, *, n, n_ctx):
    cs = S5_CHUNK
    hs = S5_HSTATE
    for i in range(cs):
        lhs_scr[:, i * S5_HALF:(i + 1) * S5_HALF] = u_ref[0, pl.ds(i, n, stride=cs), :].astype(BF16)
    lhs = lhs_scr[...]
    yacc = _dot(lhs, kt_ref[0])
    for d in range(2):
        st_scr[...] = _dot(lhs, wb_ref[d, 0])
        a = a_ref[d, 0]
        ar, ai = a[:, :hs], a[:, hs:]

        def body(t, x, d=d, ar=ar, ai=ai):
            if d == 1:
                idx = jnp.where(t < n_ctx, n_ctx - 1 - t, n + n_ctx - 1 - t)
            else:
                idx = t
            xr, xi = x
            row = st_scr[pl.ds(idx, 1), :]
            st_scr[pl.ds(idx, 1), :] = jnp.concatenate([xr, xi], axis=1)
            return (ar * xr - ai * xi + row[:, :hs], ar * xi + ai * xr + row[:, hs:])

        zero = jnp.zeros((1, hs), F32)
        lax.fori_loop(0, n, body, (zero, zero))
        yacc = yacc + _dot(st_scr[...].astype(BF16), wc_ref[d, 0])
    for j in range(cs):
        y_ref[0, pl.ds(j, n, stride=cs), :] = yacc[:, j * S5_HALF:(j + 1) * S5_HALF]


def _s5_branch(z, sp, dims):
    nb, seq, ctx = dims
    lt = seq + ctx
    n = lt // S5_CHUNK
    kw = S5_CHUNK * S5_HALF
    hs2 = 2 * S5_HSTATE
    kern = functools.partial(_s5_kernel, n=n, n_ctx=ctx // S5_CHUNK)
    return pl.pallas_call(
        kern,
        grid=(2, nb),
        in_specs=[pl.BlockSpec((1, lt, S5_HALF), lambda g, b: (b, 0, Z_S5 // S5_HALF + g)),
                  pl.BlockSpec((1, kw, kw), lambda g, b: (g, 0, 0), pipeline_mode=pl.Buffered(1)),
                  pl.BlockSpec((2, 1, kw, hs2), lambda g, b: (0, g, 0, 0), pipeline_mode=pl.Buffered(1)),
                  pl.BlockSpec((2, 1, hs2, kw), lambda g, b: (0, g, 0, 0), pipeline_mode=pl.Buffered(1)),
                  pl.BlockSpec((2, 1, 1, hs2), lambda g, b: (0, g, 0, 0))],
        out_specs=pl.BlockSpec((1, lt, S5_HALF), lambda g, b: (b, 0, g)),
        out_shape=jax.ShapeDtypeStruct((nb, lt, S5_WIDTH), F32),
        scratch_shapes=[pltpu.VMEM((n, kw), BF16), pltpu.VMEM((n, hs2), F32)],
        compiler_params=_params("parallel", "parallel"),
        name="s5_scan",
    )(z, sp['ktoep'], sp['wb'], sp['wc'], sp['a_chunk'])


def _s5_operators(p):
    cs = S5_CHUNK
    g, hch = S5_GROUPS, S5_GROUP_CH
    eye8 = jnp.eye(8, dtype=F32)
    kts, wbs, wcs, acs = 0.0, [], [], []
    for d in range(2):
        lr = jnp.minimum(p['s5_a_re'][d].astype(F32), -1e-4)
        li = p['s5_a_im'][d].astype(F32)
        dt = jnp.exp(p['s5_log_dt'][d].astype(F32))[:, None]
        tt = jnp.arange(cs + 1, dtype=F32)[:, None, None]
        mag = jnp.exp(lr * dt * tt)
        pr, pi = mag * jnp.cos(li * dt * tt), mag * jnp.sin(li * dt * tt)
        den = lr * lr + li * li
        qr = ((pr[1] - 1.0) * lr + pi[1] * li) / den
        qi = (pi[1] * lr - (pr[1] - 1.0) * li) / den
        b_re, b_im = p['s5_b_re'][d].astype(F32), p['s5_b_im'][d].astype(F32)
        bbr = qr[..., None] * b_re - qi[..., None] * b_im
        bbi = qr[..., None] * b_im + qi[..., None] * b_re
        cr, ci = p['s5_c_re'][d].astype(F32), p['s5_c_im'][d].astype(F32)
        abr = pr[:cs, :, :, None] * bbr[None] - pi[:cs, :, :, None] * bbi[None]
        abi = pr[:cs, :, :, None] * bbi[None] + pi[:cs, :, :, None] * bbr[None]
        ktau = jnp.einsum('ghn,tgni->tghi', cr, abr) - jnp.einsum('ghn,tgni->tghi', ci, abi)
        ii = jnp.arange(cs)[:, None]
        jj = jnp.arange(cs)[None, :]
        lag = (ii - jj) if d == 1 else (jj - ii)
        kts = kts + jnp.where((lag >= 0)[:, :, None, None, None], ktau[jnp.clip(lag, 0, cs - 1)], 0.0)
        e_in = jnp.arange(cs) if d == 1 else (cs - 1 - jnp.arange(cs))
        wbr, wbi = abr[e_in], abi[e_in]
        f_out = (cs - jnp.arange(cs)) if d == 1 else (jnp.arange(cs) + 1)
        wcr = cr[None] * pr[f_out][:, :, None, :] - ci[None] * pi[f_out][:, :, None, :]
        wci = cr[None] * pi[f_out][:, :, None, :] + ci[None] * pr[f_out][:, :, None, :]
        wb_h, wc_h, ac_h = [], [], []
        for half in range(2):
            gs = slice(half * 8, half * 8 + 8)
            wre = jnp.einsum('ignh,gk->ighkn', wbr[:, gs], eye8).reshape(cs * S5_HALF, S5_HSTATE)
            wim = jnp.einsum('ignh,gk->ighkn', wbi[:, gs], eye8).reshape(cs * S5_HALF, S5_HSTATE)
            wb_h.append(jnp.concatenate([wre, wim], axis=1))
            cre = jnp.einsum('jghn,gk->knjgh', wcr[:, gs], eye8).reshape(S5_HSTATE, cs * S5_HALF)
            cim = jnp.einsum('jghn,gk->knjgh', -wci[:, gs], eye8).reshape(S5_HSTATE, cs * S5_HALF)
            wc_h.append(jnp.concatenate([cre, cim], axis=0))
            ac_h.append(jnp.concatenate([pr[cs][gs].reshape(1, S5_HSTATE), pi[cs][gs].reshape(1, S5_HSTATE)], axis=1))
        wbs.append(jnp.stack(wb_h))
        wcs.append(jnp.stack(wc_h))
        acs.append(jnp.stack(ac_h))
    dsk = p['s5_d'].astype(F32).reshape(g, hch)
    eye_t = jnp.eye(cs, dtype=F32)[:, :, None, None, None]
    kts = kts + eye_t * (dsk[:, :, None] * jnp.eye(hch, dtype=F32)[None])[None, None]
    kt_h = []
    for half in range(2):
        gs = slice(half * 8, half * 8 + 8)
        kt_h.append(jnp.einsum('ijgoh,gk->ighjko', kts[:, :, gs], eye8).reshape(cs * S5_HALF, cs * S5_HALF))
    return dict(ktoep=jnp.stack(kt_h).astype(BF16), wb=jnp.stack(wbs).astype(BF16),
                wc=jnp.stack(wcs).astype(BF16), a_chunk=jnp.stack(acs))


def _merge_kernel(x_ref, m_ref, yf_ref, yr_ref, bonus_ref, gdn_ref, hf_ref, hr_ref, o_ref, s5_ref,
                  br0_ref, br1_ref, br2_ref,
                  gup_ref, gng_ref, gnb_ref, bones64_ref, glw_ref, glb_ref, mlg_ref, bones_ml_ref, mlmask_ref,
                  brb_ref, uprw_ref, ups5_ref, upml_ref, wout_ref, lng_ref, lnb_ref, out_ref, *, nb, tr, ctx_tiles):
    d = D_MODEL
    is_ctx = pl.program_id(0) < ctx_tiles
    rows = nb * tr
    flat = lambda ref: ref[...].reshape(rows, ref.shape[-1])
    ys = flat(yf_ref) + flat(yr_ref)
    b64 = bones64_ref[...]
    yc = ys - _mm_xr(ys, b64) * (1.0 / RW_HEAD_DIM)
    var = _mm_xr(yc * yc, b64) * (1.0 / RW_HEAD_DIM)
    yn = yc * lax.rsqrt(var + RW_GN_EPS) * gng_ref[...] + gnb_ref[...]
    rw_y = (yn + flat(bonus_ref)) * _mm(jax.nn.sigmoid(flat(gdn_ref)), gup_ref[...])
    s5 = jax.nn.gelu(flat(s5_ref))
    s5_y = s5 * jax.nn.sigmoid(_mm(s5, glw_ref[...]) + glb_ref[...])
    hg = jax.nn.sigmoid(flat(o_ref)) * (flat(hf_ref) + flat(hr_ref))
    bml = bones_ml_ref[...]
    hc = (hg - _mm_xr(hg, bml) * (1.0 / ML_HEAD_DIM)) * mlmask_ref[...]
    hv = _mm_xr(hc * hc, bml) * (1.0 / ML_HEAD_DIM)
    ml_y = hc * lax.rsqrt(hv + ML_EPS) * mlg_ref[...]
    brb = brb_ref[...]
    y = (jax.nn.sigmoid(flat(br0_ref) + brb[:, 0:d]) * _mm(rw_y, uprw_ref[...])
         + jax.nn.sigmoid(flat(br1_ref) + brb[:, d:2 * d]) * _mm(s5_y, ups5_ref[...])
         + jax.nn.sigmoid(flat(br2_ref) + brb[:, 2 * d:3 * d]) * _mm(ml_y, upml_ref[...]))
    out = _mm(y, wout_ref[...])
    for b in range(nb):
        gate = _mod_vec(m_ref, is_ctx, b, nb, 5)
        out_ref[b] = _layer_norm(DEEPNORM_ALPHA * x_ref[b] + gate * out[b * tr:(b + 1) * tr, :],
                                 lng_ref[...], lnb_ref[...])


def _merge(xs, mods, yf, yr, bonus, hf, hr, s5y, z, p, dims):
    nb, seq, ctx = dims
    lt = seq + ctx
    d = D_MODEL
    tr = _pick_tile((128, 64), seq, ctx)
    row = lambda width, col=0: pl.BlockSpec((nb, tr, width), lambda i: (0, i, col))
    consts = [p['g_up'], p['gn_g'], p['gn_b'], p['bones64'], p['glu_w'], p['glu_b'], p['ml_norm_g'], p['bones_ml'],
              p['ml_mask'], p['br_b'], p['up_rw'], p['up_s5'], p['up_ml'], p['w_out'], p['ln_g1'], p['ln_b1']]
    kern = functools.partial(_merge_kernel, nb=nb, tr=tr, ctx_tiles=ctx // tr)
    return pl.pallas_call(
        kern,
        grid=(lt // tr,),
        in_specs=[row(d), _const(mods.shape),
                  row(RW_WIDTH), row(RW_WIDTH), row(RW_WIDTH), row(RW_GATE_LORA, Z_GDN // RW_GATE_LORA),
                  row(ML_PW), row(ML_PW), row(ML_PW, Z_MLO // ML_PW), row(S5_WIDTH),
                  row(d, Z_BR // d), row(d, Z_BR // d + 1), row(d, Z_BR // d + 2)] + [_const(a.shape) for a in consts],
        out_specs=row(d),
        out_shape=jax.ShapeDtypeStruct((nb, lt, d), F32),
        compiler_params=_params("parallel"),
        name="merge",
    )(xs, mods, yf, yr, bonus, z, hf, hr, z, s5y, z, z, z, *consts)


def _pad_heads(w, axis):
    shape = w.shape
    w = w.reshape(shape[:axis] + (ML_HEADS, ML_HEAD_DIM) + shape[axis + 1:])
    pad = [(0, 0)] * w.ndim
    pad[axis + 1] = (0, ML_PAD - ML_HEAD_DIM)
    return jnp.pad(w, pad).reshape(shape[:axis] + (ML_PW,) + shape[axis + 1:])


def _permute_w_in(w):
    old = {}
    start = 0
    for name, size in (('rw_conv', 3 * RW_WIDTH), ('ml_q', ML_WIDTH), ('ml_k', ML_WIDTH), ('ml_v', ML_WIDTH),
                       ('ml_o', ML_WIDTH), ('ml_gl', 4 * ML_HEADS), ('s5_u', S5_WIDTH), ('w_dn', RW_DECAY_LORA),
                       ('a_dn', RW_AAA_LORA), ('g_dn', RW_GATE_LORA), ('br_gl', N_BRANCH * D_MODEL)):
        old[name] = w[:, start:start + size]
        start += size
    zeros = lambda n: jnp.zeros((w.shape[0], n), w.dtype)
    out = jnp.concatenate([old['rw_conv'], old['w_dn'], old['a_dn'], old['g_dn'], old['ml_gl'],
                           zeros(LANES - 4 * ML_HEADS), old['s5_u'], zeros(Z_MLV - Z_S5 - S5_WIDTH),
                           _pad_heads(old['ml_v'], 1), _pad_heads(old['ml_o'], 1), _pad_heads(old['ml_q'], 1),
                           _pad_heads(old['ml_k'], 1), old['br_gl']], axis=1)
    assert out.shape[1] == Z_WIDTH
    return out


def _block_ones(width, group):
    g = np.arange(width) // group
    return (g[:, None] == g[None, :]).astype(np.float32)


def _layer_params(i, a):
    w = RW_WIDTH
    pad_lo = jnp.zeros((2, RW_AAA_LORA, w), F32)
    gb = a['ml_gate_b'][i].reshape(16).astype(F32)
    ml_real = (np.arange(ML_PW) % ML_PAD < ML_HEAD_DIM).astype(np.float32)
    return dict(
        k_k=a['rw_k_k'][i].reshape(1, w), k_a=a['rw_k_a'][i].reshape(1, w), r_k=a['rw_r_k'][i].reshape(1, w),
        w0=a['rw_w0'][i].reshape(2, 1, w), a0=a['rw_a0'][i].reshape(2, 1, w),
        w_up=jnp.concatenate([a['rw_w_up'][i], pad_lo], axis=1).astype(BF16),
        a_up=jnp.concatenate([pad_lo, a['rw_a_up'][i]], axis=1).astype(BF16),
        bones64=jnp.asarray(_block_ones(w, RW_HEAD_DIM), BF16),
        bones_ml=jnp.asarray(_block_ones(ML_PW, ML_PAD), BF16),
        ml_mask=jnp.asarray(ml_real.reshape(1, ML_PW)),
        gb_row=jnp.concatenate([gb, jnp.zeros((LANES - 16,), F32)]).reshape(1, LANES), gb_col=gb.reshape(16, 1),
        g_up=a['rw_g_up'][i].astype(BF16), gn_g=a['rw_gn_g'][i].reshape(1, w), gn_b=a['rw_gn_b'][i].reshape(1, w),
        glu_w=a['s5_glu_w'][i].astype(BF16), glu_b=a['s5_glu_b'][i].reshape(1, S5_WIDTH),
        ml_norm_g=_pad_heads(a['ml_norm_g'][i].reshape(1, ML_WIDTH), 1),
        br_b=a['br_gate_b'][i].reshape(1, N_BRANCH * D_MODEL),
        up_rw=a['up_rw'][i].astype(BF16), up_s5=a['up_s5'][i].astype(BF16),
        up_ml=_pad_heads(a['up_ml'][i], 0).astype(BF16),
        w_out=a['w_out'][i].astype(BF16),
        ln_g1=a['ln_g'][i, 1].reshape(1, D_MODEL), ln_b1=a['ln_b'][i, 1].reshape(1, D_MODEL),
        conv_rw=a['conv_w'][i][:, :, :3 * RW_WIDTH],
        conv_ml=jnp.concatenate([_pad_heads(a['conv_w'][i][:, :, 3 * RW_WIDTH:3 * RW_WIDTH + ML_WIDTH], 2),
                                 _pad_heads(a['conv_w'][i][:, :, 3 * RW_WIDTH + ML_WIDTH:], 2)], axis=2),
    )


def kernel(x, c, ctx, c_ctx, ada_w, ada_b, ln_g, ln_b, ffn_w_gate, ffn_w_up, ffn_w_down, w_in, conv_w, rw_w0, rw_w_up, rw_a0, rw_a_up, rw_g_up, rw_k_k, rw_k_a, rw_r_k, rw_gn_g, rw_gn_b, s5_a_re, s5_a_im, s5_log_dt, s5_b_re, s5_b_im, s5_c_re, s5_c_im, s5_d, s5_glu_w, s5_glu_b, ml_gate_b, ml_norm_g, up_rw, up_s5, up_ml, br_gate_b, w_out):
    a = dict(ln_g=ln_g, ln_b=ln_b, conv_w=conv_w, rw_w0=rw_w0, rw_w_up=rw_w_up, rw_a0=rw_a0, rw_a_up=rw_a_up,
             rw_g_up=rw_g_up, rw_k_k=rw_k_k, rw_k_a=rw_k_a, rw_r_k=rw_r_k, rw_gn_g=rw_gn_g, rw_gn_b=rw_gn_b,
             s5_glu_w=s5_glu_w, s5_glu_b=s5_glu_b, ml_gate_b=ml_gate_b, ml_norm_g=ml_norm_g, up_rw=up_rw,
             up_s5=up_s5, up_ml=up_ml, br_gate_b=br_gate_b, w_out=w_out)
    nb, seq, d = x.shape
    nctx = ctx.shape[1]
    dims = (nb, seq, nctx)
    lt = seq + nctx
    depth = ada_w.shape[0]
    rows = -(-(nb + 1) // SUBLANES) * SUBLANES
    cv = jnp.concatenate([c, c_ctx[None, :], jnp.zeros((rows - nb - 1, d), F32)], axis=0)
    mods = _modulation(cv, ada_w, ada_b)
    xs = None
    for i in range(depth):
        m = mods[i]
        p = _layer_params(i, a)
        sp = _s5_operators(dict(s5_a_re=s5_a_re[i], s5_a_im=s5_a_im[i], s5_log_dt=s5_log_dt[i], s5_b_re=s5_b_re[i],
                                s5_b_im=s5_b_im[i], s5_c_re=s5_c_re[i], s5_c_im=s5_c_im[i], s5_d=s5_d[i]))
        w_in_p = _permute_w_in(w_in[i].astype(BF16))
        ffn_w = lambda j: (ffn_w_gate[i, j].astype(BF16), ffn_w_up[i, j].astype(BF16), ffn_w_down[i, j].astype(BF16))
        tr = _ffn_tile(dims)
        ct, st = nctx // tr, seq // tr
        ln0 = (ln_g[i, 0], ln_b[i, 0])
        if i == 0:
            xs = _ffn_half(x, m, 0, *ffn_w(0), *ln0, dims, src_tile0=0, n_tiles=st, n_ctx_tiles=0,
                           dst_rows=lt, dst_tile0=ct)
            xs = _ffn_half(ctx, m, 0, *ffn_w(0), *ln0, dims, src_tile0=0, n_tiles=ct, n_ctx_tiles=ct,
                           dst_rows=lt, dst_tile0=0, dst=xs)
        else:
            xs = _ffn_half(xs, m, 0, *ffn_w(0), *ln0, dims, src_tile0=0, n_tiles=ct + st, n_ctx_tiles=ct,
                           dst_rows=lt, dst_tile0=0)
        z = _in_proj(xs, m, w_in_p, dims)
        zc_rw = _short_conv(z, p['conv_rw'], Z_RWC, dims)
        zc_ml = _short_conv(z, p['conv_ml'], Z_MLC, dims)
        ops, bonus = _rwkv_chunk(zc_rw, z, p, dims)
        yf, yr = _rwkv_scan(ops, dims)
        glt = z[:, :, Z_MLG:Z_MLG + 16].reshape(nb, lt // CHUNK, CHUNK, 16).transpose(0, 1, 3, 2)
        hf, hr = _mlstm(zc_ml, z, glt, p, dims)
        s5y = _s5_branch(z, sp, dims)
        xs = _merge(xs, m, yf, yr, bonus, hf, hr, s5y, z, p, dims)
        if i == depth - 1:
            xs = _ffn_half(xs, m, 6, *ffn_w(1), ln_g[i, 2], ln_b[i, 2], dims, src_tile0=ct, n_tiles=st,
                           n_ctx_tiles=0, dst_rows=seq, dst_tile0=0)
        else:
            xs = _ffn_half(xs, m, 6, *ffn_w(1), ln_g[i, 2], ln_b[i, 2], dims, src_tile0=0, n_tiles=ct + st,
                           n_ctx_tiles=ct, dst_rows=lt, dst_tile0=0)
    return xs
```

```python
import functools

import numpy as np
import jax
import jax.numpy as jnp
from jax import lax
from jax.experimental import pallas as pl
from jax.experimental.pallas import tpu as pltpu

F32 = jnp.float32
BF16 = jnp.bfloat16

D_MODEL = 1024
DEPTH = 2
GRID_W = 64

RW_HEADS = 6
RW_HEAD_DIM = 64
RW_WIDTH = 384
RW_DECAY_LORA = 64
RW_AAA_LORA = 64
RW_GATE_LORA = 128
RW_GN_EPS = 64e-5

S5_GROUPS = 16
S5_GROUP_CH = 16
S5_WIDTH = 256
S5_STATE = 64

ML_HEADS = 4
ML_HEAD_DIM = 96
ML_WIDTH = 384
ML_EPS = 1e-5

D_FF = 2816
N_BRANCH = 3
N_MOD = 9
LN_EPS = 1e-5
DEEPNORM_ALPHA = (2.0 * DEPTH) ** 0.25

LANES = 128
SUBLANES = 8
MXU_COLS = 256

CHUNK = 64
S5_CHUNK = 8
S5_HALF = 128
S5_HSTATE = 512
ML_PAD = LANES
ML_PW = ML_HEADS * ML_PAD

Z_RWC = 0
Z_LORA = Z_RWC + 3 * RW_WIDTH
Z_GDN = Z_LORA + LANES
Z_MLG = Z_GDN + RW_GATE_LORA
Z_S5 = Z_MLG + LANES
Z_MLV = 2048
Z_MLO = Z_MLV + ML_PW
Z_MLC = Z_MLO + ML_PW
Z_BR = Z_MLC + 2 * ML_PW
Z_WIDTH = Z_BR + N_BRANCH * D_MODEL

VMEM_LIMIT = 58 * 1024 * 1024

_NN = (((1,), (0,)), ((), ()))
_NT = (((1,), (1,)), ((), ()))
_TN = (((0,), (0,)), ((), ()))


def _dot(a, b, dims=_NN):
    return lax.dot_general(a, b, dims, preferred_element_type=F32)


def _mm(a, b, dims=_NN):
    return _dot(a.astype(BF16), b.astype(BF16), dims)


def _split(a, n):
    parts, r = [], a
    for _ in range(n):
        p = r.astype(BF16)
        parts.append(p)
        r = r - p.astype(F32)
    return parts


def _mm_xr(a, b_exact, dims=_NN):
    ps = _split(a, 3)
    return _dot(ps[0], b_exact, dims) + (_dot(ps[1], b_exact, dims) + _dot(ps[2], b_exact, dims))


def _mm_xl(a_exact, b, dims=_NN):
    ps = _split(b, 3)
    return _dot(a_exact, ps[0], dims) + (_dot(a_exact, ps[1], dims) + _dot(a_exact, ps[2], dims))


def _mm3(a, b, dims=_NN):
    ah, al = _split(a, 2)
    bh, bl = _split(b, 2)
    return _dot(ah, bh, dims) + (_dot(al, bh, dims) + _dot(ah, bl, dims))


def _softplus(x):
    return jnp.maximum(x, 0.0) + jnp.log1p(jnp.exp(-jnp.abs(x)))


def _layer_norm(y, g, b):
    yc = y - jnp.mean(y, axis=-1, keepdims=True)
    return yc * lax.rsqrt(jnp.mean(yc * yc, axis=-1, keepdims=True) + LN_EPS) * g + b


def _pick_tile(cands, *lens):
    for t in cands:
        if all(n % t == 0 for n in lens):
            return t
    raise ValueError(f"no tile in {cands} divides {lens}")


def _params(*sem):
    return pltpu.CompilerParams(dimension_semantics=sem, vmem_limit_bytes=VMEM_LIMIT)


def _const(shape):
    zeros = (0,) * len(shape)
    return pl.BlockSpec(shape, lambda *_: zeros, pipeline_mode=pl.Buffered(1))


def _mod_vec(m_ref, is_ctx, b, nb, k):
    row = jnp.where(is_ctx, nb, b)
    return m_ref[pl.ds(row, 1), k * D_MODEL:(k + 1) * D_MODEL]


def _mod_kernel(cv_ref, w_ref, b_ref, o_ref):
    s = cv_ref[...]
    s = s * jax.nn.sigmoid(s)
    o_ref[0] = _mm3(s, w_ref[0]) + b_ref[0]


def _modulation(cv, ada_w, ada_b):
    depth, d, nd = ada_w.shape
    rows = cv.shape[0]
    return pl.pallas_call(
        _mod_kernel,
        grid=(depth, nd // d),
        in_specs=[pl.BlockSpec((rows, d), lambda l, j: (0, 0)),
                  pl.BlockSpec((1, d, d), lambda l, j: (l, 0, j)),
                  pl.BlockSpec((1, 1, d), lambda l, j: (l, 0, j))],
        out_specs=pl.BlockSpec((1, rows, d), lambda l, j: (l, 0, j)),
        out_shape=jax.ShapeDtypeStruct((depth, rows, nd), F32),
        compiler_params=_params("parallel", "parallel"),
        name="modulation",
    )(cv, ada_w, ada_b.reshape(depth, 1, nd))


def _ffn_kernel(x_ref, m_ref, wg_ref, wu_ref, wd_ref, g_ref, b_ref, *rest, k0, nb, tr, n_ctx_tiles):
    o_ref, u_scr, h_scr = rest[-3:]
    is_ctx = pl.program_id(0) < n_ctx_tiles
    x_in = (lambda b: jnp.where(is_ctx, rest[0][b], x_ref[b])) if len(rest) == 4 else (lambda b: x_ref[b])
    for b in range(nb):
        shift = _mod_vec(m_ref, is_ctx, b, nb, k0)
        scale = _mod_vec(m_ref, is_ctx, b, nb, k0 + 1)
        u_scr[b * tr:(b + 1) * tr, :] = (x_in(b) * (1.0 + scale) + shift).astype(BF16)
    u = u_scr[...]
    for f in range(D_FF // MXU_COLS):
        cols = slice(f * MXU_COLS, (f + 1) * MXU_COLS)
        g = _dot(u, wg_ref[:, cols])
        up = _dot(u, wu_ref[:, cols])
        h_scr[:, cols] = (g * jax.nn.sigmoid(g) * up).astype(BF16)
    acc = _dot(h_scr[...], wd_ref[...])
    for b in range(nb):
        gate = _mod_vec(m_ref, is_ctx, b, nb, k0 + 2)
        y = DEEPNORM_ALPHA * x_in(b) + (0.5 * gate) * acc[b * tr:(b + 1) * tr, :]
        o_ref[b] = _layer_norm(y, g_ref[...], b_ref[...])


def _ffn_half(src, mods, k0, wg, wu, wd, g, b, dims, *, src_tile0, n_tiles, n_ctx_tiles, dst_rows, ctx_src=None):
    nb, seq, ctx = dims
    d = D_MODEL
    tr = _ffn_tile(dims)
    kern = functools.partial(_ffn_kernel, k0=k0, nb=nb, tr=tr, n_ctx_tiles=n_ctx_tiles)
    in_specs = [pl.BlockSpec((nb, tr, d), lambda i: (0, i + src_tile0, 0)),
                _const(mods.shape), _const((d, D_FF)), _const((d, D_FF)), _const((D_FF, d)),
                _const((1, d)), _const((1, d))]
    args = [src, mods, wg, wu, wd, g.reshape(1, d), b.reshape(1, d)]
    if ctx_src is not None:
        assert n_ctx_tiles == 1 and src_tile0 == 0
        in_specs[0] = pl.BlockSpec((nb, tr, d), lambda i: (0, jnp.maximum(i - 1, 0), 0))
        in_specs.append(_const((nb, tr, d)))
        args.append(ctx_src)
    return pl.pallas_call(
        kern,
        grid=(n_tiles,),
        in_specs=in_specs,
        out_specs=pl.BlockSpec((nb, tr, d), lambda i: (0, i, 0)),
        out_shape=jax.ShapeDtypeStruct((nb, dst_rows, d), F32),
        scratch_shapes=[pltpu.VMEM((nb * tr, d), BF16), pltpu.VMEM((nb * tr, D_FF), BF16)],
        compiler_params=_params("parallel"),
        name="ffn_half",
    )(*args)


def _ffn_tile(dims):
    return _pick_tile((256, 128, 64), dims[1], dims[2])


def _inproj_kernel(x_ref, m_ref, w_ref, o_ref, u_scr, *, nb, tr, ctx_tiles):
    is_ctx = pl.program_id(0) < ctx_tiles

    @pl.when(pl.program_id(1) == 0)
    def _():
        for b in range(nb):
            shift = _mod_vec(m_ref, is_ctx, b, nb, 3)
            scale = _mod_vec(m_ref, is_ctx, b, nb, 4)
            u_scr[b * tr:(b + 1) * tr, :] = (x_ref[b] * (1.0 + scale) + shift).astype(BF16)

    res = _dot(u_scr[...], w_ref[...])
    for b in range(nb):
        o_ref[b] = res[b * tr:(b + 1) * tr, :]


def _in_proj(xs, mods, w_in_p, dims):
    nb, seq, ctx = dims
    d = D_MODEL
    lt = seq + ctx
    tr = _pick_tile((256, 128, 64), seq, ctx)
    tn = 7 * MXU_COLS
    kern = functools.partial(_inproj_kernel, nb=nb, tr=tr, ctx_tiles=ctx // tr)
    return pl.pallas_call(
        kern,
        grid=(lt // tr, Z_WIDTH // tn),
        in_specs=[pl.BlockSpec((nb, tr, d), lambda i, j: (0, i, 0)),
                  pl.BlockSpec(mods.shape, lambda i, j: (0, 0)),
                  pl.BlockSpec((d, tn), lambda i, j: (0, j))],
        out_specs=pl.BlockSpec((nb, tr, tn), lambda i, j: (0, i, j)),
        out_shape=jax.ShapeDtypeStruct((nb, lt, Z_WIDTH), F32),
        scratch_shapes=[pltpu.VMEM((nb * tr, d), BF16)],
        compiler_params=_params("parallel", "arbitrary"),
        name="in_proj",
    )(xs, mods, w_in_p)


def _conv_kernel(prev_ref, cur_ref, next_ref, w_ref, o_ref, *, tmc, ctx, lt):
    row0 = pl.program_id(1) * tmc
    is_lat = row0 >= ctx
    has_prev = row0 > ctx
    has_next = jnp.logical_and(is_lat, row0 + tmc < lt)
    pos = lax.broadcasted_iota(jnp.int32, (tmc, 1), 0)
    col = lax.rem(pos, GRID_W)
    keep = lambda drop: jnp.where(drop, 0.0, 1.0)
    m_left = jnp.where(is_lat, keep(col == 0), keep(jnp.logical_and(pos == 0, row0 == 0)))
    m_right = jnp.where(is_lat, keep(col == GRID_W - 1),
                        keep(jnp.logical_and(pos == tmc - 1, row0 + tmc == ctx)))
    cur = cur_ref[0]
    prev = jnp.where(has_prev, prev_ref[0], 0.0)
    nxt = jnp.where(has_next, next_ref[0], 0.0)
    if tmc > GRID_W:
        up = jnp.concatenate([prev, cur[:tmc - GRID_W]], axis=0)
        dn = jnp.concatenate([cur[GRID_W:], nxt], axis=0)
    else:
        up, dn = prev, nxt
    lat_f = jnp.where(is_lat, 1.0, 0.0)
    w = w_ref[...]
    w_up = w[0] * lat_f
    w_dn = w[2] * lat_f

    def tap(j):
        return up * w_up[j:j + 1, :] + cur * w[1, j:j + 1, :] + dn * w_dn[j:j + 1, :]

    o_ref[0] = m_left * pltpu.roll(tap(0), 1, 0) + tap(1) + m_right * pltpu.roll(tap(2), tmc - 1, 0)


def _short_conv(z, conv_w, col0, dims):
    nb, seq, ctx = dims
    lt = seq + ctx
    cw = conv_w.shape[-1]
    tmc = _pick_tile((256, 128, 64), seq, ctx)
    assert tmc == ctx, "the context sequence conv expects one tile per context segment"
    hb = tmc // GRID_W
    nblk = lt // GRID_W
    cb = col0 // cw
    kern = functools.partial(_conv_kernel, tmc=tmc, ctx=ctx, lt=lt)
    return pl.pallas_call(
        kern,
        grid=(nb, lt // tmc),
        in_specs=[pl.BlockSpec((1, GRID_W, cw), lambda b, i: (b, jnp.maximum(i * hb - 1, 0), cb)),
                  pl.BlockSpec((1, tmc, cw), lambda b, i: (b, i, cb)),
                  pl.BlockSpec((1, GRID_W, cw), lambda b, i: (b, jnp.minimum(i * hb + hb, nblk - 1), cb)),
                  pl.BlockSpec((3, 3, cw), lambda b, i: (0, 0, 0))],
        out_specs=pl.BlockSpec((1, tmc, cw), lambda b, i: (b, i, 0)),
        out_shape=jax.ShapeDtypeStruct((nb, lt, cw), F32),
        compiler_params=_params("parallel", "parallel"),
        name="short_conv",
    )(z, z, z, conv_w)


def _stream_chunk(s, rev, nc, nl):
    if not rev:
        return s
    return jnp.where(s < nc, nc - 1 - s, 2 * nc + nl - 1 - s)


def _pair_masks():
    lane = lax.broadcasted_iota(jnp.int32, (1, LANES), 1)
    m0 = (lane < RW_HEAD_DIM).astype(F32)
    return m0, 1.0 - m0


def _make_bd(m0, m1):
    m0h, m1h = m0.astype(BF16), m1.astype(BF16)

    def bd(y):
        y = y.astype(BF16)
        n = y.shape[1] // LANES
        a0 = jnp.concatenate([m0h] * n, axis=1) if n > 1 else m0h
        a1 = jnp.concatenate([m1h] * n, axis=1) if n > 1 else m1h
        return jnp.concatenate([y * a0, y * a1], axis=0)

    return bd


def _rwkv_kernel(rf, kf, vf, lof, rr, kr, vr, lor, kk_ref, ka_ref, rk_ref, w0_ref, wup_ref, a0_ref, aup_ref,
                 bones_ref, yf_ref, yr_ref, bonus_ref, st_scr, *, nb):
    c = CHUNK
    bones = bones_ref[...]
    m0, m1 = _pair_masks()
    bd = _make_bd(m0, m1)

    @pl.when(pl.program_id(0) == 0)
    def _():
        st_scr[...] = jnp.zeros_like(st_scr)

    ti = lax.broadcasted_iota(jnp.int32, (c, c), 0)
    tj = lax.broadcasted_iota(jnp.int32, (c, c), 1)
    pt_i = lax.broadcasted_iota(jnp.int32, (c, LANES), 0)
    pt_j = lax.rem(lax.broadcasted_iota(jnp.int32, (c, LANES), 1), c)
    eye_p = (pt_i == pt_j).astype(F32)

    chains = []
    for d, (r_ref, k_ref, v_ref, lo_ref, y_ref) in enumerate(((rf, kf, vf, lof, yf_ref), (rr, kr, vr, lor, yr_ref))):
        rev = d == 1
        tri = ((tj >= ti) if rev else (tj <= ti)).astype(BF16)
        strict = ((pt_j > pt_i) if rev else (pt_j < pt_i)).astype(F32)
        incl = strict + eye_p
        for b in range(nb):
            r = r_ref[b]
            k = k_ref[b]
            v = v_ref[b]
            lo = lo_ref[b]
            kk = k * kk_ref[...]
            kk = kk * lax.rsqrt(jnp.maximum(_mm_xr(kk * kk, bones), 1e-24))
            tw = jnp.tanh(lo)
            a_of = lambda dd: jax.nn.sigmoid(a0_ref[dd] + _mm(lo, aup_ref[dd]))
            kd_of = lambda a: k * (1.0 + (a - 1.0) * ka_ref[...])
            wl = w0_ref[d] + _mm(tw, wup_ref[d])
            lw = -jnp.exp(-_softplus(-wl) - 0.5)
            a = a_of(d)
            kd = kd_of(a)
            bb = kk * a
            if d == 0:
                bonus_ref[b] = _mm_xr(r * (kd + kd_of(a_of(1))) * rk_ref[...], bones) * v
            cs = _mm_xl(tri, lw)
            csp = cs - lw
            tot = cs[0:1, :] if rev else cs[c - 1:c, :]
            mid = 0.5 * tot
            e_dn = jnp.exp(mid - cs)
            a_hat = kk * jnp.exp(csp - mid)
            r_hat = r * jnp.exp(cs - mid)
            b_hat = bb * e_dn
            k_hat = kd * e_dn
            a_st = kk * jnp.exp(csp)
            r_st = r * jnp.exp(cs)
            e_end = jnp.exp(tot - cs)
            b_end = bb * e_end
            k_end = kd * e_end
            w_c = jnp.exp(tot)
            for p in range(RW_HEADS // 2):
                sl = slice(p * LANES, (p + 1) * LANES)
                chains.append(dict(d=d, b=b, sl=sl, y_ref=y_ref, strict=strict, incl=incl, vp=v[:, sl],
                                   a_hat=a_hat[:, sl], r_hat=r_hat[:, sl], b_hat=b_hat[:, sl], k_hat=k_hat[:, sl],
                                   a_st=a_st[:, sl], r_st=r_st[:, sl], b_end=b_end[:, sl], k_end=k_end[:, sl],
                                   w_c=w_c[:, sl]))

    for ch in chains:
        lhs = jnp.concatenate([ch['a_hat'], ch['r_hat']], axis=0).astype(BF16)
        rhs_t = jnp.concatenate([bd(ch['b_hat']), bd(ch['k_hat'])], axis=0)
        g4 = _dot(lhs, rhs_t, _NT)
        ch['lp'] = g4[0:c, 0:LANES] * ch['strict']
        ch['t'] = eye_p - ch['lp']
        ch['m_rb'] = (g4[c:, 0:LANES] * ch['incl']).astype(BF16)
        ch['lak_mrk'] = jnp.concatenate([g4[0:c, LANES:] * ch['strict'], g4[c:, LANES:] * ch['incl']],
                                        axis=0).astype(BF16)
    for ch in chains:
        ch['lp'] = _dot(ch['lp'].astype(BF16), bd(ch['lp']))
    for it in range(5):
        for ch in chains:
            if it < 4:
                res = _dot(jnp.concatenate([ch['t'], ch['lp']], axis=0).astype(BF16), bd(ch['lp']))
                ch['t'] = ch['t'] + res[0:c]
                ch['lp'] = res[c:]
            else:
                ch['t'] = ch['t'] + _dot(ch['t'].astype(BF16), bd(ch['lp']))
    for ch in chains:
        xv = _dot(ch['lak_mrk'], bd(ch['vp']))
        ch['x0'] = xv[0:c]
        ch['mrk_v'] = xv[c:]
    for ch in chains:
        ch['ua'] = _dot(ch['t'].astype(BF16), bd(jnp.concatenate([ch['x0'], ch['a_st']], axis=1)))
    for ch in chains:
        mu = _dot(ch['m_rb'], bd(ch['ua']))
        ch['y1'] = ch['mrk_v'] - mu[:, 0:LANES]
        ch['rq'] = ch['r_st'] - mu[:, LANES:]
    for ch in chains:
        ua = ch['ua']
        rhs = jnp.concatenate([jnp.concatenate([ch['vp'], jnp.zeros_like(ch['vp'])], axis=1),
                               jnp.concatenate([-ua[:, 0:LANES], ua[:, LANES:]], axis=1)], axis=0)
        hg = _mm(jnp.concatenate([ch['k_end'], ch['b_end']], axis=0), rhs, _TN)
        ch['pt'] = eye_p * ch['w_c'] - (hg[0:c, LANES:] * m0 + hg[c:, LANES:] * m1)
        ch['qt'] = hg[0:c, 0:LANES] * m0 + hg[c:, 0:LANES] * m1
    for ch in chains:
        d, b, sl = ch['d'], ch['b'], ch['sl']
        res = _dot(jnp.concatenate([ch['rq'], ch['pt']], axis=0).astype(BF16), bd(st_scr[d, b, :, sl]))
        ch['y_ref'][b, :, sl] = ch['y1'] + res[0:c]
        st_scr[d, b, :, sl] = res[c:] + ch['qt']


def _rwkv(zc, z, p, dims):
    nb, seq, ctx = dims
    lt = seq + ctx
    w = RW_WIDTH
    c = CHUNK
    nc, nl = ctx // c, seq // c

    def specs(rev):
        ck = lambda s: _stream_chunk(s, rev, nc, nl)
        return [pl.BlockSpec((nb, c, w), lambda s: (0, ck(s), 0)),
                pl.BlockSpec((nb, c, w), lambda s: (0, ck(s), 1)),
                pl.BlockSpec((nb, c, w), lambda s: (0, ck(s), 2)),
                pl.BlockSpec((nb, c, LANES), lambda s: (0, ck(s), Z_LORA // LANES))]

    consts = [p['k_k'], p['k_a'], p['r_k'], p['w0'], p['w_up'], p['a0'], p['a_up'], p['bones64']]
    out = jax.ShapeDtypeStruct((nb, lt, w), F32)
    fwd = pl.BlockSpec((nb, c, w), lambda s: (0, _stream_chunk(s, False, nc, nl), 0))
    return pl.pallas_call(
        functools.partial(_rwkv_kernel, nb=nb),
        grid=(nc + nl,),
        in_specs=specs(False) + specs(True) + [_const(a.shape) for a in consts],
        out_specs=[fwd, pl.BlockSpec((nb, c, w), lambda s: (0, _stream_chunk(s, True, nc, nl), 0)), fwd],
        out_shape=[out, out, out],
        scratch_shapes=[pltpu.VMEM((2, nb, c, w), F32)],
        compiler_params=_params("arbitrary"),
        name="rwkv",
    )(zc, zc, zc, z, zc, zc, zc, z, *consts)


def _mlstm_kernel(qf, kf, vf, gcf, gtf, qr, kr, vr, gcr, gtr, gbr_ref, gbc_ref, sel_ref, hrow_ref, hf_ref, hr_ref,
                  c_scr, n_scr, m_scr, *, nb):
    c = CHUNK
    nh, dh, pw = ML_HEADS, ML_HEAD_DIM, ML_PAD

    @pl.when(pl.program_id(0) == 0)
    def _():
        c_scr[...] = jnp.zeros_like(c_scr)
        n_scr[...] = jnp.zeros_like(n_scr)
        m_scr[...] = jnp.zeros_like(m_scr)

    lane1 = lax.broadcasted_iota(jnp.int32, (1, LANES), 1)
    ti = lax.broadcasted_iota(jnp.int32, (c, c), 0)
    tj = lax.broadcasted_iota(jnp.int32, (c, c), 1)
    eye = (ti == tj).astype(BF16)
    hrow = hrow_ref[...]

    chains = []
    for d, (q_ref, k_ref, v_ref, gc_ref, gt_ref, h_ref) in enumerate(
            ((qf, kf, vf, gcf, gtf, hf_ref), (qr, kr, vr, gcr, gtr, hr_ref))):
        rev = d == 1
        tri = ((tj >= ti) if rev else (tj <= ti)).astype(BF16)
        valid_t = (ti >= tj) if rev else (ti <= tj)
        ig_lanes = jnp.logical_and(lane1 >= d * 8, lane1 < d * 8 + nh).astype(F32)
        fg_lanes = jnp.logical_and(lane1 >= d * 8 + nh, lane1 < d * 8 + 2 * nh).astype(F32)
        for b in range(nb):
            q = q_ref[b]
            q = (q * jax.nn.sigmoid(q)).astype(BF16)
            k = k_ref[b]
            k = (k * jax.nn.sigmoid(k) * (dh ** -0.5)).astype(BF16)
            v = v_ref[b].astype(BF16)
            gc = gc_ref[b] + gbr_ref[...]
            gt = gt_ref[b, 0] + gbc_ref[...]
            bc_cols = _mm_xl(tri, -_softplus(-gc))
            bc_rows = _mm_xr(-_softplus(-gt), tri, _NT)
            cm = _mm_xr(gc * ig_lanes - bc_cols * fg_lanes, sel_ref[d])
            n_prev = n_scr[d, b]
            chains.append(dict(d=d, b=b, rev=rev, valid_t=valid_t, q=q, k=k, v=v, gt=gt, bc_rows=bc_rows, cm=cm,
                               qt=_dot(q, eye, _TN), kt=_dot(k, eye, _TN),
                               nq=_dot((hrow * n_prev).astype(BF16), q, _NT),
                               n_prev=n_prev, h_ref=h_ref, m_all=m_scr[d, b]))
    for ch in chains:
        d, rev = ch['d'], ch['rev']
        heads = []
        for h in range(nh):
            hs = slice(h * pw, (h + 1) * pw)
            ci, cf = d * 8 + h, d * 8 + nh + h
            bcr = ch['bc_rows'][cf:cf + 1, :]
            icr = ch['gt'][ci:ci + 1, :]
            b_last = bcr[:, 0:1] if rev else bcr[:, c - 1:c]
            m_prev = ch['m_all'][:, h:h + 1]
            lt = jnp.where(ch['valid_t'], bcr + ch['cm'][:, h * c:(h + 1) * c], -jnp.inf)
            log_inter = bcr + m_prev
            m = jnp.maximum(log_inter, jnp.max(lt, axis=0, keepdims=True))
            inter = jnp.exp(log_inter - m)
            sm_t = _dot(ch['k'][:, hs], ch['q'][:, hs], _NT) * jnp.exp(lt - m)
            den = inter * ch['nq'][h:h + 1, :] + jnp.sum(sm_t, axis=0, keepdims=True)
            w1 = 1.0 / jnp.maximum(jnp.abs(den), jnp.exp(-m))
            log_w = b_last - bcr + icr
            m_new = jnp.maximum(b_last + m_prev, jnp.max(log_w, axis=-1, keepdims=True))
            heads.append(dict(hs=hs, sm_w=(sm_t * w1).astype(BF16), qt_w=(ch['qt'][hs, :] * (inter * w1)).astype(BF16),
                              m_new=m_new, wgt=jnp.exp(log_w - m_new), dec=jnp.exp(b_last + m_prev - m_new)))
        ch['heads'] = heads
    for ch in chains:
        d, b = ch['d'], ch['b']
        for h, hd in enumerate(ch['heads']):
            hs = hd['hs']
            lhs_t = jnp.concatenate([hd['sm_w'], hd['qt_w']], axis=0)
            rhs = jnp.concatenate([ch['v'][:, hs], c_scr[d, b, h].astype(BF16)], axis=0)
            ch['h_ref'][b, :, hs] = _dot(lhs_t, rhs, _TN)
    for ch in chains:
        d, b = ch['d'], ch['b']
        m_new_all = jnp.zeros((1, LANES), F32)
        wrows = jnp.zeros((SUBLANES, c), F32)
        dec_e = jnp.zeros((1, ML_PW), F32)
        sub = lax.broadcasted_iota(jnp.int32, (SUBLANES, 1), 0)
        for h, hd in enumerate(ch['heads']):
            hs = hd['hs']
            kt_w = (ch['kt'][hs, :] * hd['wgt']).astype(BF16)
            c_scr[d, b, h] = hd['dec'] * c_scr[d, b, h] + _dot(kt_w, ch['v'][:, hs])
            m_new_all = m_new_all + hd['m_new'] * (lane1 == h).astype(F32)
            wrows = wrows + hd['wgt'] * (sub == h).astype(F32)
            dec_e = dec_e + hd['dec'] * hrow[h:h + 1, :]
        n_add = jnp.sum(_dot(wrows.astype(BF16), ch['k']) * hrow, axis=0, keepdims=True)
        n_scr[d, b] = dec_e * ch['n_prev'] + n_add
        m_scr[d, b] = m_new_all


def _mlstm(zc_ml, z, glt, p, dims):
    nb, seq, ctx = dims
    lt = seq + ctx
    c = CHUNK
    nc, nl = ctx // c, seq // c
    pw = ML_PW

    def specs(rev):
        ck = lambda s: _stream_chunk(s, rev, nc, nl)
        return [pl.BlockSpec((nb, c, pw), lambda s: (0, ck(s), 0)),
                pl.BlockSpec((nb, c, pw), lambda s: (0, ck(s), 1)),
                pl.BlockSpec((nb, c, pw), lambda s: (0, ck(s), Z_MLV // pw)),
                pl.BlockSpec((nb, c, LANES), lambda s: (0, ck(s), Z_MLG // LANES)),
                pl.BlockSpec((nb, 1, 16, c), lambda s: (0, ck(s), 0, 0))]

    out = jax.ShapeDtypeStruct((nb, lt, pw), F32)
    consts = [p['gb_row'], p['gb_col'], p['ml_sel'], p['ml_hrow']]
    return pl.pallas_call(
        functools.partial(_mlstm_kernel, nb=nb),
        grid=(nc + nl,),
        in_specs=specs(False) + specs(True) + [_const(a.shape) for a in consts],
        out_specs=[pl.BlockSpec((nb, c, pw), lambda s: (0, _stream_chunk(s, False, nc, nl), 0)),
                   pl.BlockSpec((nb, c, pw), lambda s: (0, _stream_chunk(s, True, nc, nl), 0))],
        out_shape=[out, out],
        scratch_shapes=[pltpu.VMEM((2, nb, ML_HEADS, ML_PAD, ML_PAD), F32), pltpu.VMEM((2, nb, 1, pw), F32),
                        pltpu.VMEM((2, nb, 1, LANES), F32)],
        compiler_params=_params("arbitrary"),
        name="mlstm",
    )(zc_ml, zc_ml, z, z, glt, zc_ml, zc_ml, z, z, glt, *consts)


def _s5_kernel(u_ref, kt_ref, wb_ref, wc_ref, a_ref, y_ref, lhs_scr, st_scr, *, n, n_ctx):
    cs = S5_CHUNK
    hs = S5_HSTATE
    for i in range(cs):
        lhs_scr[:, i * S5_HALF:(i + 1) * S5_HALF] = u_ref[0, pl.ds(i, n, stride=cs), :].astype(BF16)
    lhs = lhs_scr[...]
    yacc = _dot(lhs, kt_ref[0])
    for d in range(2):
        st_scr[...] = _dot(lhs, wb_ref[d, 0])
        a = a_ref[d, 0]
        ar, ai = a[:, :hs], a[:, hs:]

        def body(t, x, d=d, ar=ar, ai=ai):
            if d == 1:
                idx = jnp.where(t < n_ctx, n_ctx - 1 - t, n + n_ctx - 1 - t)
            else:
                idx = t
            xr, xi = x
            row = st_scr[pl.ds(idx, 1), :]
            st_scr[pl.ds(idx, 1), :] = jnp.concatenate([xr, xi], axis=1)
            return (ar * xr - ai * xi + row[:, :hs], ar * xi + ai * xr + row[:, hs:])

        zero = jnp.zeros((1, hs), F32)
        lax.fori_loop(0, n, body, (zero, zero))
        yacc = yacc + _dot(st_scr[...].astype(BF16), wc_ref[d, 0])
    for j in range(cs):
        y_ref[0, pl.ds(j, n, stride=cs), :] = yacc[:, j * S5_HALF:(j + 1) * S5_HALF]


def _s5_branch(z, sp, dims):
    nb, seq, ctx = dims
    lt = seq + ctx
    n = lt // S5_CHUNK
    kw = S5_CHUNK * S5_HALF
    hs2 = 2 * S5_HSTATE
    kern = functools.partial(_s5_kernel, n=n, n_ctx=ctx // S5_CHUNK)
    once = pl.Buffered(1)

    def half(g):
        return pl.pallas_call(
            kern,
            grid=(nb,),
            in_specs=[pl.BlockSpec((1, lt, S5_HALF), lambda b: (b, 0, Z_S5 // S5_HALF + g)),
                      pl.BlockSpec((1, kw, kw), lambda b: (g, 0, 0), pipeline_mode=once),
                      pl.BlockSpec((2, 1, kw, hs2), lambda b: (0, g, 0, 0), pipeline_mode=once),
                      pl.BlockSpec((2, 1, hs2, kw), lambda b: (0, g, 0, 0), pipeline_mode=once),
                      pl.BlockSpec((2, 1, 1, hs2), lambda b: (0, g, 0, 0), pipeline_mode=once)],
            out_specs=pl.BlockSpec((1, lt, S5_HALF), lambda b: (b, 0, 0)),
            out_shape=jax.ShapeDtypeStruct((nb, lt, S5_HALF), F32),
            scratch_shapes=[pltpu.VMEM((n, kw), BF16), pltpu.VMEM((n, hs2), F32)],
            compiler_params=_params("parallel"),
            name="s5_scan",
        )(z, sp['ktoep'], sp['wb'], sp['wc'], sp['a_chunk'])

    return half(0), half(1)


def _s5_operators(p):
    cs = S5_CHUNK
    g, hch = S5_GROUPS, S5_GROUP_CH
    eye8 = jnp.eye(8, dtype=F32)
    kts, wbs, wcs, acs = 0.0, [], [], []
    for d in range(2):
        lr = jnp.minimum(p['s5_a_re'][d].astype(F32), -1e-4)
        li = p['s5_a_im'][d].astype(F32)
        dt = jnp.exp(p['s5_log_dt'][d].astype(F32))[:, None]
        tt = jnp.arange(cs + 1, dtype=F32)[:, None, None]
        mag = jnp.exp(lr * dt * tt)
        pr, pi = mag * jnp.cos(li * dt * tt), mag * jnp.sin(li * dt * tt)
        den = lr * lr + li * li
        qr = ((pr[1] - 1.0) * lr + pi[1] * li) / den
        qi = (pi[1] * lr - (pr[1] - 1.0) * li) / den
        b_re, b_im = p['s5_b_re'][d].astype(F32), p['s5_b_im'][d].astype(F32)
        bbr = qr[..., None] * b_re - qi[..., None] * b_im
        bbi = qr[..., None] * b_im + qi[..., None] * b_re
        cr, ci = p['s5_c_re'][d].astype(F32), p['s5_c_im'][d].astype(F32)
        abr = pr[:cs, :, :, None] * bbr[None] - pi[:cs, :, :, None] * bbi[None]
        abi = pr[:cs, :, :, None] * bbi[None] + pi[:cs, :, :, None] * bbr[None]
        ktau = jnp.einsum('ghn,tgni->tghi', cr, abr) - jnp.einsum('ghn,tgni->tghi', ci, abi)
        ii = jnp.arange(cs)[:, None]
        jj = jnp.arange(cs)[None, :]
        lag = (ii - jj) if d == 1 else (jj - ii)
        kts = kts + jnp.where((lag >= 0)[:, :, None, None, None], ktau[jnp.clip(lag, 0, cs - 1)], 0.0)
        e_in = jnp.arange(cs) if d == 1 else (cs - 1 - jnp.arange(cs))
        wbr, wbi = abr[e_in], abi[e_in]
        f_out = (cs - jnp.arange(cs)) if d == 1 else (jnp.arange(cs) + 1)
        wcr = cr[None] * pr[f_out][:, :, None, :] - ci[None] * pi[f_out][:, :, None, :]
        wci = cr[None] * pi[f_out][:, :, None, :] + ci[None] * pr[f_out][:, :, None, :]
        wb_h, wc_h, ac_h = [], [], []
        for half in range(2):
            gs = slice(half * 8, half * 8 + 8)
            wre = jnp.einsum('ignh,gk->ighkn', wbr[:, gs], eye8).reshape(cs * S5_HALF, S5_HSTATE)
            wim = jnp.einsum('ignh,gk->ighkn', wbi[:, gs], eye8).reshape(cs * S5_HALF, S5_HSTATE)
            wb_h.append(jnp.concatenate([wre, wim], axis=1))
            cre = jnp.einsum('jghn,gk->knjgh', wcr[:, gs], eye8).reshape(S5_HSTATE, cs * S5_HALF)
            cim = jnp.einsum('jghn,gk->knjgh', -wci[:, gs], eye8).reshape(S5_HSTATE, cs * S5_HALF)
            wc_h.append(jnp.concatenate([cre, cim], axis=0))
            ac_h.append(jnp.concatenate([pr[cs][gs].reshape(1, S5_HSTATE), pi[cs][gs].reshape(1, S5_HSTATE)], axis=1))
        wbs.append(jnp.stack(wb_h))
        wcs.append(jnp.stack(wc_h))
        acs.append(jnp.stack(ac_h))
    dsk = p['s5_d'].astype(F32).reshape(g, hch)
    eye_t = jnp.eye(cs, dtype=F32)[:, :, None, None, None]
    kts = kts + eye_t * (dsk[:, :, None] * jnp.eye(hch, dtype=F32)[None])[None, None]
    kt_h = []
    for half in range(2):
        gs = slice(half * 8, half * 8 + 8)
        kt_h.append(jnp.einsum('ijgoh,gk->ighjko', kts[:, :, gs], eye8).reshape(cs * S5_HALF, cs * S5_HALF))
    return dict(ktoep=jnp.stack(kt_h).astype(BF16), wb=jnp.stack(wbs).astype(BF16),
                wc=jnp.stack(wcs).astype(BF16), a_chunk=jnp.stack(acs))


def _merge_kernel(x_ref, m_ref, yf_ref, yr_ref, bonus_ref, gdn_ref, hf_ref, hr_ref, o_ref, s5a_ref, s5b_ref,
                  br0_ref, br1_ref, br2_ref,
                  gup_ref, gng_ref, gnb_ref, bones64_ref, glw_ref, glb_ref, mlg_ref, bones_ml_ref, mlmask_ref,
                  brb_ref, uprw_ref, ups5_ref, upml_ref, wout_ref, lng_ref, lnb_ref, out_ref, *, nb, tr, ctx_tiles):
    d = D_MODEL
    is_ctx = pl.program_id(0) < ctx_tiles
    rows = nb * tr
    flat = lambda ref: ref[...].reshape(rows, ref.shape[-1])
    ys = flat(yf_ref) + flat(yr_ref)
    b64 = bones64_ref[...]
    yc = ys - _mm_xr(ys, b64) * (1.0 / RW_HEAD_DIM)
    var = _mm_xr(yc * yc, b64) * (1.0 / RW_HEAD_DIM)
    yn = yc * lax.rsqrt(var + RW_GN_EPS) * gng_ref[...] + gnb_ref[...]
    rw_y = (yn + flat(bonus_ref)) * _mm(jax.nn.sigmoid(flat(gdn_ref)), gup_ref[...])
    s5 = jax.nn.gelu(jnp.concatenate([flat(s5a_ref), flat(s5b_ref)], axis=1))
    s5_y = s5 * jax.nn.sigmoid(_mm(s5, glw_ref[...]) + glb_ref[...])
    hg = jax.nn.sigmoid(flat(o_ref)) * (flat(hf_ref) + flat(hr_ref))
    bml = bones_ml_ref[...]
    hc = (hg - _mm_xr(hg, bml) * (1.0 / ML_HEAD_DIM)) * mlmask_ref[...]
    hv = _mm_xr(hc * hc, bml) * (1.0 / ML_HEAD_DIM)
    ml_y = hc * lax.rsqrt(hv + ML_EPS) * mlg_ref[...]
    brb = brb_ref[...]
    y = (jax.nn.sigmoid(flat(br0_ref) + brb[:, 0:d]) * _mm(rw_y, uprw_ref[...])
         + jax.nn.sigmoid(flat(br1_ref) + brb[:, d:2 * d]) * _mm(s5_y, ups5_ref[...])
         + jax.nn.sigmoid(flat(br2_ref) + brb[:, 2 * d:3 * d]) * _mm(ml_y, upml_ref[...]))
    out = _mm(y, wout_ref[...])
    for b in range(nb):
        gate = _mod_vec(m_ref, is_ctx, b, nb, 5)
        out_ref[b] = _layer_norm(DEEPNORM_ALPHA * x_ref[b] + gate * out[b * tr:(b + 1) * tr, :],
                                 lng_ref[...], lnb_ref[...])


def _merge(xs, mods, yf, yr, bonus, hf, hr, s5y, z, p, dims):
    nb, seq, ctx = dims
    lt = seq + ctx
    d = D_MODEL
    tr = _pick_tile((128, 64), seq, ctx)
    row = lambda width, col=0: pl.BlockSpec((nb, tr, width), lambda i: (0, i, col))
    consts = [p['g_up'], p['gn_g'], p['gn_b'], p['bones64'], p['glu_w'], p['glu_b'], p['ml_norm_g'], p['bones_ml'],
              p['ml_mask'], p['br_b'], p['up_rw'], p['up_s5'], p['up_ml'], p['w_out'], p['ln_g1'], p['ln_b1']]
    kern = functools.partial(_merge_kernel, nb=nb, tr=tr, ctx_tiles=ctx // tr)
    return pl.pallas_call(
        kern,
        grid=(lt // tr,),
        in_specs=[row(d), _const(mods.shape),
                  row(RW_WIDTH), row(RW_WIDTH), row(RW_WIDTH), row(RW_GATE_LORA, Z_GDN // RW_GATE_LORA),
                  row(ML_PW), row(ML_PW), row(ML_PW, Z_MLO // ML_PW), row(S5_HALF), row(S5_HALF),
                  row(d, Z_BR // d), row(d, Z_BR // d + 1), row(d, Z_BR // d + 2)] + [_const(a.shape) for a in consts],
        out_specs=row(d),
        out_shape=jax.ShapeDtypeStruct((nb, lt, d), F32),
        compiler_params=_params("parallel"),
        name="merge",
    )(xs, mods, yf, yr, bonus, z, hf, hr, z, s5y[0], s5y[1], z, z, z, *consts)


def _pad_heads(w, axis):
    shape = w.shape
    w = w.reshape(shape[:axis] + (ML_HEADS, ML_HEAD_DIM) + shape[axis + 1:])
    pad = [(0, 0)] * w.ndim
    pad[axis + 1] = (0, ML_PAD - ML_HEAD_DIM)
    return jnp.pad(w, pad).reshape(shape[:axis] + (ML_PW,) + shape[axis + 1:])


def _permute_w_in(w):
    old = {}
    start = 0
    for name, size in (('rw_conv', 3 * RW_WIDTH), ('ml_q', ML_WIDTH), ('ml_k', ML_WIDTH), ('ml_v', ML_WIDTH),
                       ('ml_o', ML_WIDTH), ('ml_gl', 4 * ML_HEADS), ('s5_u', S5_WIDTH), ('w_dn', RW_DECAY_LORA),
                       ('a_dn', RW_AAA_LORA), ('g_dn', RW_GATE_LORA), ('br_gl', N_BRANCH * D_MODEL)):
        old[name] = w[:, start:start + size]
        start += size
    zeros = lambda n: jnp.zeros((w.shape[0], n), w.dtype)
    out = jnp.concatenate([old['rw_conv'], old['w_dn'], old['a_dn'], old['g_dn'], old['ml_gl'],
                           zeros(LANES - 4 * ML_HEADS), old['s5_u'], zeros(Z_MLV - Z_S5 - S5_WIDTH),
                           _pad_heads(old['ml_v'], 1), _pad_heads(old['ml_o'], 1), _pad_heads(old['ml_q'], 1),
                           _pad_heads(old['ml_k'], 1), old['br_gl']], axis=1)
    assert out.shape[1] == Z_WIDTH
    return out


def _ml_gate_select():
    sel = np.zeros((2, LANES, ML_HEADS * CHUNK), np.float32)
    for d in range(2):
        for h in range(ML_HEADS):
            sel[d, d * 8 + h, h * CHUNK:(h + 1) * CHUNK] = 1.0
            sel[d, d * 8 + ML_HEADS + h, h * CHUNK:(h + 1) * CHUNK] = 1.0
    return sel


def _ml_head_rows():
    rows = np.zeros((SUBLANES, ML_PW), np.float32)
    for h in range(ML_HEADS):
        rows[h, h * ML_PAD:(h + 1) * ML_PAD] = 1.0
    return rows


def _block_ones(width, group):
    g = np.arange(width) // group
    return (g[:, None] == g[None, :]).astype(np.float32)


def _layer_params(i, a):
    w = RW_WIDTH
    pad_lo = jnp.zeros((2, RW_AAA_LORA, w), F32)
    gb = a['ml_gate_b'][i].reshape(16).astype(F32)
    ml_real = (np.arange(ML_PW) % ML_PAD < ML_HEAD_DIM).astype(np.float32)
    return dict(
        k_k=a['rw_k_k'][i].reshape(1, w), k_a=a['rw_k_a'][i].reshape(1, w), r_k=a['rw_r_k'][i].reshape(1, w),
        w0=a['rw_w0'][i].reshape(2, 1, w), a0=a['rw_a0'][i].reshape(2, 1, w),
        w_up=jnp.concatenate([a['rw_w_up'][i], pad_lo], axis=1).astype(BF16),
        a_up=jnp.concatenate([pad_lo, a['rw_a_up'][i]], axis=1).astype(BF16),
        bones64=jnp.asarray(_block_ones(w, RW_HEAD_DIM), BF16),
        bones_ml=jnp.asarray(_block_ones(ML_PW, ML_PAD), BF16),
        ml_mask=jnp.asarray(ml_real.reshape(1, ML_PW)),
        ml_sel=jnp.asarray(_ml_gate_select(), BF16), ml_hrow=jnp.asarray(_ml_head_rows()),
        gb_row=jnp.concatenate([gb, jnp.zeros((LANES - 16,), F32)]).reshape(1, LANES), gb_col=gb.reshape(16, 1),
        g_up=a['rw_g_up'][i].astype(BF16), gn_g=a['rw_gn_g'][i].reshape(1, w), gn_b=a['rw_gn_b'][i].reshape(1, w),
        glu_w=a['s5_glu_w'][i].astype(BF16), glu_b=a['s5_glu_b'][i].reshape(1, S5_WIDTH),
        ml_norm_g=_pad_heads(a['ml_norm_g'][i].reshape(1, ML_WIDTH), 1),
        br_b=a['br_gate_b'][i].reshape(1, N_BRANCH * D_MODEL),
        up_rw=a['up_rw'][i].astype(BF16), up_s5=a['up_s5'][i].astype(BF16),
        up_ml=_pad_heads(a['up_ml'][i], 0).astype(BF16),
        w_out=a['w_out'][i].astype(BF16),
        ln_g1=a['ln_g'][i, 1].reshape(1, D_MODEL), ln_b1=a['ln_b'][i, 1].reshape(1, D_MODEL),
        conv_rw=a['conv_w'][i][:, :, :3 * RW_WIDTH],
        conv_ml=jnp.concatenate([_pad_heads(a['conv_w'][i][:, :, 3 * RW_WIDTH:3 * RW_WIDTH + ML_WIDTH], 2),
                                 _pad_heads(a['conv_w'][i][:, :, 3 * RW_WIDTH + ML_WIDTH:], 2)], axis=2),
    )


def kernel(x, c, ctx, c_ctx, ada_w, ada_b, ln_g, ln_b, ffn_w_gate, ffn_w_up, ffn_w_down, w_in, conv_w, rw_w0, rw_w_up, rw_a0, rw_a_up, rw_g_up, rw_k_k, rw_k_a, rw_r_k, rw_gn_g, rw_gn_b, s5_a_re, s5_a_im, s5_log_dt, s5_b_re, s5_b_im, s5_c_re, s5_c_im, s5_d, s5_glu_w, s5_glu_b, ml_gate_b, ml_norm_g, up_rw, up_s5, up_ml, br_gate_b, w_out):
    a = dict(ln_g=ln_g, ln_b=ln_b, conv_w=conv_w, rw_w0=rw_w0, rw_w_up=rw_w_up, rw_a0=rw_a0, rw_a_up=rw_a_up,
             rw_g_up=rw_g_up, rw_k_k=rw_k_k, rw_k_a=rw_k_a, rw_r_k=rw_r_k, rw_gn_g=rw_gn_g, rw_gn_b=rw_gn_b,
             s5_glu_w=s5_glu_w, s5_glu_b=s5_glu_b, ml_gate_b=ml_gate_b, ml_norm_g=ml_norm_g, up_rw=up_rw,
             up_s5=up_s5, up_ml=up_ml, br_gate_b=br_gate_b, w_out=w_out)
    nb, seq, d = x.shape
    nctx = ctx.shape[1]
    dims = (nb, seq, nctx)
    lt = seq + nctx
    depth = ada_w.shape[0]
    rows = -(-(nb + 1) // SUBLANES) * SUBLANES
    cv = jnp.concatenate([c, c_ctx[None, :], jnp.zeros((rows - nb - 1, d), F32)], axis=0)
    mods = _modulation(cv, ada_w, ada_b)
    xs = None
    for i in range(depth):
        m = mods[i]
        p = _layer_params(i, a)
        sp = _s5_operators(dict(s5_a_re=s5_a_re[i], s5_a_im=s5_a_im[i], s5_log_dt=s5_log_dt[i], s5_b_re=s5_b_re[i],
                                s5_b_im=s5_b_im[i], s5_c_re=s5_c_re[i], s5_c_im=s5_c_im[i], s5_d=s5_d[i]))
        w_in_p = _permute_w_in(w_in[i].astype(BF16))
        ffn_w = lambda j: (ffn_w_gate[i, j].astype(BF16), ffn_w_up[i, j].astype(BF16), ffn_w_down[i, j].astype(BF16))
        tr = _ffn_tile(dims)
        ct, st = nctx // tr, seq // tr
        ln0 = (ln_g[i, 0], ln_b[i, 0])
        if i == 0:
            xs = _ffn_half(x, m, 0, *ffn_w(0), *ln0, dims, src_tile0=0, n_tiles=ct + st, n_ctx_tiles=ct,
                           dst_rows=lt, ctx_src=ctx)
        else:
            xs = _ffn_half(xs, m, 0, *ffn_w(0), *ln0, dims, src_tile0=0, n_tiles=ct + st, n_ctx_tiles=ct,
                           dst_rows=lt)
        z = _in_proj(xs, m, w_in_p, dims)
        zc_rw = _short_conv(z, p['conv_rw'], Z_RWC, dims)
        zc_ml = _short_conv(z, p['conv_ml'], Z_MLC, dims)
        yf, yr, bonus = _rwkv(zc_rw, z, p, dims)
        glt = z[:, :, Z_MLG:Z_MLG + 16].reshape(nb, lt // CHUNK, CHUNK, 16).transpose(0, 1, 3, 2)
        hf, hr = _mlstm(zc_ml, z, glt, p, dims)
        s5y = _s5_branch(z, sp, dims)
        xs = _merge(xs, m, yf, yr, bonus, hf, hr, s5y, z, p, dims)
        if i == depth - 1:
            xs = _ffn_half(xs, m, 6, *ffn_w(1), ln_g[i, 2], ln_b[i, 2], dims, src_tile0=ct, n_tiles=st,
                           n_ctx_tiles=0, dst_rows=seq)
        else:
            xs = _ffn_half(xs, m, 6, *ffn_w(1), ln_g[i, 2], ln_b[i, 2], dims, src_tile0=0, n_tiles=ct + st,
                           n_ctx_tiles=ct, dst_rows=lt)
    return xs
```

```python
import functools

import numpy as np
import jax
import jax.numpy as jnp
from jax import lax
from jax.experimental import pallas as pl
from jax.experimental.pallas import tpu as pltpu

F32 = jnp.float32
BF16 = jnp.bfloat16

D_MODEL = 1024
DEPTH = 2
GRID_W = 64

RW_HEADS = 6
RW_HEAD_DIM = 64
RW_WIDTH = 384
RW_DECAY_LORA = 64
RW_AAA_LORA = 64
RW_GATE_LORA = 128
RW_GN_EPS = 64e-5

S5_GROUPS = 16
S5_GROUP_CH = 16
S5_WIDTH = 256
S5_STATE = 64

ML_HEADS = 4
ML_HEAD_DIM = 96
ML_WIDTH = 384
ML_EPS = 1e-5

D_FF = 2816
N_BRANCH = 3
N_MOD = 9
LN_EPS = 1e-5
DEEPNORM_ALPHA = (2.0 * DEPTH) ** 0.25

LANES = 128
SUBLANES = 8
MXU_COLS = 256

CHUNK = 64
S5_CHUNK = 8
S5_HALF = 128
S5_HSTATE = 512
ML_PAD = LANES
ML_PW = ML_HEADS * ML_PAD

Z_RWC = 0
Z_LORA = Z_RWC + 3 * RW_WIDTH
Z_GDN = Z_LORA + LANES
Z_MLG = Z_GDN + RW_GATE_LORA
Z_S5 = Z_MLG + LANES
Z_MLV = 2048
Z_MLO = Z_MLV + ML_PW
Z_MLC = Z_MLO + ML_PW
Z_BR = Z_MLC + 2 * ML_PW
Z_WIDTH = Z_BR + N_BRANCH * D_MODEL

VMEM_LIMIT = 58 * 1024 * 1024

_NN = (((1,), (0,)), ((), ()))
_NT = (((1,), (1,)), ((), ()))
_TN = (((0,), (0,)), ((), ()))


def _dot(a, b, dims=_NN):
    return lax.dot_general(a, b, dims, preferred_element_type=F32)


def _mm(a, b, dims=_NN):
    return _dot(a.astype(BF16), b.astype(BF16), dims)


def _split(a, n):
    parts, r = [], a
    for _ in range(n):
        p = r.astype(BF16)
        parts.append(p)
        r = r - p.astype(F32)
    return parts


def _mm_xr(a, b_exact, dims=_NN):
    hi, lo = _split(a, 2)
    m = a.shape[0]
    r = _dot(jnp.concatenate([hi, lo], axis=0), b_exact, dims)
    return r[:m] + r[m:]


def _mm_xl(a_exact, b, dims=_NN):
    hi, lo = _split(b, 2)
    return _dot(a_exact, hi, dims) + _dot(a_exact, lo, dims)


def _mm3(a, b, dims=_NN):
    ah, al = _split(a, 2)
    bh, bl = _split(b, 2)
    return _dot(ah, bh, dims) + (_dot(al, bh, dims) + _dot(ah, bl, dims))


def _softplus(x):
    return jnp.maximum(x, 0.0) + jnp.log1p(jnp.exp(-jnp.abs(x)))


def _layer_norm(y, g, b):
    yc = y - jnp.mean(y, axis=-1, keepdims=True)
    return yc * lax.rsqrt(jnp.mean(yc * yc, axis=-1, keepdims=True) + LN_EPS) * g + b


def _pick_tile(cands, *lens):
    for t in cands:
        if all(n % t == 0 for n in lens):
            return t
    raise ValueError(f"no tile in {cands} divides {lens}")


def _params(*sem):
    return pltpu.CompilerParams(dimension_semantics=sem, vmem_limit_bytes=VMEM_LIMIT)


def _const(shape):
    zeros = (0,) * len(shape)
    return pl.BlockSpec(shape, lambda *_: zeros, pipeline_mode=pl.Buffered(1))


def _mod_vec(m_ref, is_ctx, b, nb, k):
    row = jnp.where(is_ctx, nb, b)
    return m_ref[pl.ds(row, 1), k * D_MODEL:(k + 1) * D_MODEL]


def _mod_kernel(cv_ref, w_ref, b_ref, o_ref):
    s = cv_ref[...]
    s = s * jax.nn.sigmoid(s)
    o_ref[0] = _mm3(s, w_ref[0]) + b_ref[0]


def _modulation(cv, ada_w, ada_b):
    depth, d, nd = ada_w.shape
    rows = cv.shape[0]
    return pl.pallas_call(
        _mod_kernel,
        grid=(depth, nd // d),
        in_specs=[pl.BlockSpec((rows, d), lambda l, j: (0, 0)),
                  pl.BlockSpec((1, d, d), lambda l, j: (l, 0, j)),
                  pl.BlockSpec((1, 1, d), lambda l, j: (l, 0, j))],
        out_specs=pl.BlockSpec((1, rows, d), lambda l, j: (l, 0, j)),
        out_shape=jax.ShapeDtypeStruct((depth, rows, nd), F32),
        compiler_params=_params("parallel", "parallel"),
        name="modulation",
    )(cv, ada_w, ada_b.reshape(depth, 1, nd))


def _ffn_kernel(x_ref, m_ref, wg_ref, wu_ref, wd_ref, g_ref, b_ref, *rest, k0, nb, tr, n_ctx_tiles):
    o_ref, u_scr, h_scr = rest[-3:]
    is_ctx = pl.program_id(0) < n_ctx_tiles
    x_in = (lambda b: jnp.where(is_ctx, rest[0][b], x_ref[b])) if len(rest) == 4 else (lambda b: x_ref[b])
    for b in range(nb):
        shift = _mod_vec(m_ref, is_ctx, b, nb, k0)
        scale = _mod_vec(m_ref, is_ctx, b, nb, k0 + 1)
        u_scr[b * tr:(b + 1) * tr, :] = (x_in(b) * (1.0 + scale) + shift).astype(BF16)
    u = u_scr[...]
    for f in range(D_FF // MXU_COLS):
        cols = slice(f * MXU_COLS, (f + 1) * MXU_COLS)
        g = _dot(u, wg_ref[0, 0, :, cols])
        up = _dot(u, wu_ref[0, 0, :, cols])
        h_scr[:, cols] = (g * jax.nn.sigmoid(g) * up).astype(BF16)
    acc = _dot(h_scr[...], wd_ref[0, 0])
    for b in range(nb):
        gate = _mod_vec(m_ref, is_ctx, b, nb, k0 + 2)
        y = DEEPNORM_ALPHA * x_in(b) + (0.5 * gate) * acc[b * tr:(b + 1) * tr, :]
        o_ref[b] = _layer_norm(y, g_ref[...], b_ref[...])


def _ffn_half(src, mods, k0, wg, wu, wd, wi, g, b, dims, *, src_tile0, n_tiles, n_ctx_tiles, dst_rows,
              ctx_src=None):
    nb, seq, ctx = dims
    d = D_MODEL
    tr = _ffn_tile(dims)
    kern = functools.partial(_ffn_kernel, k0=k0, nb=nb, tr=tr, n_ctx_tiles=n_ctx_tiles)
    weight = lambda r, c: pl.BlockSpec((1, 1, r, c), lambda i: (wi[0], wi[1], 0, 0), pipeline_mode=pl.Buffered(1))
    in_specs = [pl.BlockSpec((nb, tr, d), lambda i: (0, i + src_tile0, 0)),
                _const(mods.shape), weight(d, D_FF), weight(d, D_FF), weight(D_FF, d),
                _const((1, d)), _const((1, d))]
    args = [src, mods, wg, wu, wd, g.reshape(1, d), b.reshape(1, d)]
    if ctx_src is not None:
        assert n_ctx_tiles == 1 and src_tile0 == 0
        in_specs[0] = pl.BlockSpec((nb, tr, d), lambda i: (0, jnp.maximum(i - 1, 0), 0))
        in_specs.append(_const((nb, tr, d)))
        args.append(ctx_src)
    return pl.pallas_call(
        kern,
        grid=(n_tiles,),
        in_specs=in_specs,
        out_specs=pl.BlockSpec((nb, tr, d), lambda i: (0, i, 0)),
        out_shape=jax.ShapeDtypeStruct((nb, dst_rows, d), F32),
        scratch_shapes=[pltpu.VMEM((nb * tr, d), BF16), pltpu.VMEM((nb * tr, D_FF), BF16)],
        compiler_params=_params("parallel"),
        name="ffn_half",
    )(*args)


def _ffn_tile(dims):
    return _pick_tile((256, 128, 64), dims[1], dims[2])


def _inproj_kernel(x_ref, m_ref, w_ref, wgt_ref, o_ref, gt_ref, u_scr, *, nb, tr, ctx_tiles):
    is_ctx = pl.program_id(0) < ctx_tiles

    @pl.when(pl.program_id(1) == 0)
    def _():
        for b in range(nb):
            shift = _mod_vec(m_ref, is_ctx, b, nb, 3)
            scale = _mod_vec(m_ref, is_ctx, b, nb, 4)
            u_scr[b * tr:(b + 1) * tr, :] = (x_ref[b] * (1.0 + scale) + shift).astype(BF16)
        gt = _dot(wgt_ref[...], u_scr[...], _NT)
        for b in range(nb):
            for q in range(tr // CHUNK):
                t0 = b * tr + q * CHUNK
                gt_ref[b, q] = gt[0:4 * ML_HEADS, t0:t0 + CHUNK]

    res = _dot(u_scr[...], w_ref[...])
    for b in range(nb):
        o_ref[b] = res[b * tr:(b + 1) * tr, :]


def _in_proj(xs, mods, w_in_p, w_gate_t, dims):
    nb, seq, ctx = dims
    d = D_MODEL
    lt = seq + ctx
    tr = _pick_tile((256, 128, 64), seq, ctx)
    tn = 7 * MXU_COLS
    ng = 4 * ML_HEADS
    kern = functools.partial(_inproj_kernel, nb=nb, tr=tr, ctx_tiles=ctx // tr)
    return pl.pallas_call(
        kern,
        grid=(lt // tr, Z_WIDTH // tn),
        in_specs=[pl.BlockSpec((nb, tr, d), lambda i, j: (0, i, 0)),
                  pl.BlockSpec(mods.shape, lambda i, j: (0, 0)),
                  pl.BlockSpec((d, tn), lambda i, j: (0, j)),
                  pl.BlockSpec(w_gate_t.shape, lambda i, j: (0, 0))],
        out_specs=[pl.BlockSpec((nb, tr, tn), lambda i, j: (0, i, j)),
                   pl.BlockSpec((nb, tr // CHUNK, ng, CHUNK), lambda i, j: (0, i, 0, 0))],
        out_shape=[jax.ShapeDtypeStruct((nb, lt, Z_WIDTH), F32),
                   jax.ShapeDtypeStruct((nb, lt // CHUNK, ng, CHUNK), F32)],
        scratch_shapes=[pltpu.VMEM((nb * tr, d), BF16)],
        compiler_params=_params("parallel", "arbitrary"),
        name="in_proj",
    )(xs, mods, w_in_p, w_gate_t)


def _conv_kernel(prev_ref, cur_ref, next_ref, w_ref, o_ref, *, tmc, ctx, lt):
    row0 = pl.program_id(1) * tmc
    is_lat = row0 >= ctx
    has_prev = row0 > ctx
    has_next = jnp.logical_and(is_lat, row0 + tmc < lt)
    pos = lax.broadcasted_iota(jnp.int32, (tmc, 1), 0)
    col = lax.rem(pos, GRID_W)
    keep = lambda drop: jnp.where(drop, 0.0, 1.0)
    m_left = jnp.where(is_lat, keep(col == 0), keep(jnp.logical_and(pos == 0, row0 == 0)))
    m_right = jnp.where(is_lat, keep(col == GRID_W - 1),
                        keep(jnp.logical_and(pos == tmc - 1, row0 + tmc == ctx)))
    cur = cur_ref[0]
    prev = jnp.where(has_prev, prev_ref[0], 0.0)
    nxt = jnp.where(has_next, next_ref[0], 0.0)
    if tmc > GRID_W:
        up = jnp.concatenate([prev, cur[:tmc - GRID_W]], axis=0)
        dn = jnp.concatenate([cur[GRID_W:], nxt], axis=0)
    else:
        up, dn = prev, nxt
    lat_f = jnp.where(is_lat, 1.0, 0.0)
    w = w_ref[...]
    w_up = w[0] * lat_f
    w_dn = w[2] * lat_f

    def tap(j):
        return up * w_up[j:j + 1, :] + cur * w[1, j:j + 1, :] + dn * w_dn[j:j + 1, :]

    o_ref[0] = m_left * pltpu.roll(tap(0), 1, 0) + tap(1) + m_right * pltpu.roll(tap(2), tmc - 1, 0)


def _short_conv(z, conv_w, col0, dims):
    nb, seq, ctx = dims
    lt = seq + ctx
    cw = conv_w.shape[-1]
    tmc = _pick_tile((256, 128, 64), seq, ctx)
    assert tmc == ctx, "the context sequence conv expects one tile per context segment"
    hb = tmc // GRID_W
    nblk = lt // GRID_W
    cb = col0 // cw
    kern = functools.partial(_conv_kernel, tmc=tmc, ctx=ctx, lt=lt)
    return pl.pallas_call(
        kern,
        grid=(nb, lt // tmc),
        in_specs=[pl.BlockSpec((1, GRID_W, cw), lambda b, i: (b, jnp.maximum(i * hb - 1, 0), cb)),
                  pl.BlockSpec((1, tmc, cw), lambda b, i: (b, i, cb)),
                  pl.BlockSpec((1, GRID_W, cw), lambda b, i: (b, jnp.minimum(i * hb + hb, nblk - 1), cb)),
                  pl.BlockSpec((3, 3, cw), lambda b, i: (0, 0, 0))],
        out_specs=pl.BlockSpec((1, tmc, cw), lambda b, i: (b, i, 0)),
        out_shape=jax.ShapeDtypeStruct((nb, lt, cw), F32),
        compiler_params=_params("parallel", "parallel"),
        name="short_conv",
    )(z, z, z, conv_w)


def _stream_chunk(s, rev, nc, nl):
    if not rev:
        return s
    return jnp.where(s < nc, nc - 1 - s, 2 * nc + nl - 1 - s)


def _pair_masks():
    lane = lax.broadcasted_iota(jnp.int32, (1, LANES), 1)
    m0 = (lane < RW_HEAD_DIM).astype(F32)
    return m0, 1.0 - m0


def _make_bd(m0, m1):
    m0h, m1h = m0.astype(BF16), m1.astype(BF16)

    def bd(y):
        y = y.astype(BF16)
        n = y.shape[1] // LANES
        a0 = jnp.concatenate([m0h] * n, axis=1) if n > 1 else m0h
        a1 = jnp.concatenate([m1h] * n, axis=1) if n > 1 else m1h
        return jnp.concatenate([y * a0, y * a1], axis=0)

    return bd


def _rwkv_kernel(rf, kf, vf, lof, rr, kr, vr, lor, kk_ref, ka_ref, rk_ref, w0_ref, wup_ref, a0_ref, aup_ref,
                 bones_ref, yf_ref, yr_ref, bonus_ref, st_scr, *, nb):
    c = CHUNK
    bones = bones_ref[...]
    m0, m1 = _pair_masks()
    bd = _make_bd(m0, m1)

    @pl.when(pl.program_id(0) == 0)
    def _():
        st_scr[...] = jnp.zeros_like(st_scr)

    ti = lax.broadcasted_iota(jnp.int32, (c, c), 0)
    tj = lax.broadcasted_iota(jnp.int32, (c, c), 1)
    pt_i = lax.broadcasted_iota(jnp.int32, (c, LANES), 0)
    pt_j = lax.rem(lax.broadcasted_iota(jnp.int32, (c, LANES), 1), c)
    eye_p = (pt_i == pt_j).astype(F32)

    chains = []
    for d, (r_ref, k_ref, v_ref, lo_ref, y_ref) in enumerate(((rf, kf, vf, lof, yf_ref), (rr, kr, vr, lor, yr_ref))):
        rev = d == 1
        tri = ((tj >= ti) if rev else (tj <= ti)).astype(BF16)
        strict = ((pt_j > pt_i) if rev else (pt_j < pt_i)).astype(F32)
        incl = strict + eye_p
        for b in range(nb):
            r = r_ref[b]
            k = k_ref[b]
            v = v_ref[b]
            lo = lo_ref[b]
            kk = k * kk_ref[...]
            kk = kk * lax.rsqrt(jnp.maximum(_mm_xr(kk * kk, bones), 1e-24))
            tw = jnp.tanh(lo)
            a_of = lambda dd: jax.nn.sigmoid(a0_ref[dd] + _mm(lo, aup_ref[dd]))
            kd_of = lambda a: k * (1.0 + (a - 1.0) * ka_ref[...])
            wl = w0_ref[d] + _mm(tw, wup_ref[d])
            lw = -jnp.exp(-_softplus(-wl) - 0.5)
            a = a_of(d)
            kd = kd_of(a)
            bb = kk * a
            if d == 0:
                bonus_ref[b] = _mm_xr(r * (kd + kd_of(a_of(1))) * rk_ref[...], bones) * v
            cs = _mm_xl(tri, lw)
            csp = cs - lw
            tot = cs[0:1, :] if rev else cs[c - 1:c, :]
            mid = 0.5 * tot
            e_dn = jnp.exp(mid - cs)
            a_hat = kk * jnp.exp(csp - mid)
            r_hat = r * jnp.exp(cs - mid)
            b_hat = bb * e_dn
            k_hat = kd * e_dn
            a_st = kk * jnp.exp(csp)
            r_st = r * jnp.exp(cs)
            e_end = jnp.exp(tot - cs)
            b_end = bb * e_end
            k_end = kd * e_end
            w_c = jnp.exp(tot)
            for p in range(RW_HEADS // 2):
                sl = slice(p * LANES, (p + 1) * LANES)
                chains.append(dict(d=d, b=b, sl=sl, y_ref=y_ref, strict=strict, incl=incl, vp=v[:, sl],
                                   a_hat=a_hat[:, sl], r_hat=r_hat[:, sl], b_hat=b_hat[:, sl], k_hat=k_hat[:, sl],
                                   a_st=a_st[:, sl], r_st=r_st[:, sl], b_end=b_end[:, sl], k_end=k_end[:, sl],
                                   w_c=w_c[:, sl]))

    for ch in chains:
        lhs = jnp.concatenate([ch['a_hat'], ch['r_hat']], axis=0).astype(BF16)
        rhs_t = jnp.concatenate([bd(ch['b_hat']), bd(ch['k_hat'])], axis=0)
        g4 = _dot(lhs, rhs_t, _NT)
        ch['lp'] = g4[0:c, 0:LANES] * ch['strict']
        ch['t'] = eye_p - ch['lp']
        ch['m_rb'] = (g4[c:, 0:LANES] * ch['incl']).astype(BF16)
        ch['lak_mrk'] = jnp.concatenate([g4[0:c, LANES:] * ch['strict'], g4[c:, LANES:] * ch['incl']],
                                        axis=0).astype(BF16)
    for ch in chains:
        ch['lp'] = _dot(ch['lp'].astype(BF16), bd(ch['lp']))
    for it in range(5):
        for ch in chains:
            if it < 4:
                res = _dot(jnp.concatenate([ch['t'], ch['lp']], axis=0).astype(BF16), bd(ch['lp']))
                ch['t'] = ch['t'] + res[0:c]
                ch['lp'] = res[c:]
            else:
                ch['t'] = ch['t'] + _dot(ch['t'].astype(BF16), bd(ch['lp']))
    for ch in chains:
        xv = _dot(ch['lak_mrk'], bd(ch['vp']))
        ch['x0'] = xv[0:c]
        ch['mrk_v'] = xv[c:]
    for ch in chains:
        ch['ua'] = _dot(ch['t'].astype(BF16), bd(jnp.concatenate([ch['x0'], ch['a_st']], axis=1)))
    for ch in chains:
        mu = _dot(ch['m_rb'], bd(ch['ua']))
        ch['y1'] = ch['mrk_v'] - mu[:, 0:LANES]
        ch['rq'] = ch['r_st'] - mu[:, LANES:]
    for ch in chains:
        ua = ch['ua']
        rhs = jnp.concatenate([jnp.concatenate([ch['vp'], jnp.zeros_like(ch['vp'])], axis=1),
                               jnp.concatenate([-ua[:, 0:LANES], ua[:, LANES:]], axis=1)], axis=0)
        hg = _mm(jnp.concatenate([ch['k_end'], ch['b_end']], axis=0), rhs, _TN)
        ch['pt'] = eye_p * ch['w_c'] - (hg[0:c, LANES:] * m0 + hg[c:, LANES:] * m1)
        ch['qt'] = hg[0:c, 0:LANES] * m0 + hg[c:, 0:LANES] * m1
    for ch in chains:
        d, b, sl = ch['d'], ch['b'], ch['sl']
        res = _dot(jnp.concatenate([ch['rq'], ch['pt']], axis=0).astype(BF16), bd(st_scr[d, b, :, sl]))
        ch['y_ref'][b, :, sl] = ch['y1'] + res[0:c]
        st_scr[d, b, :, sl] = res[c:] + ch['qt']


def _rwkv(zc, z, p, dims):
    nb, seq, ctx = dims
    lt = seq + ctx
    w = RW_WIDTH
    c = CHUNK
    nc, nl = ctx // c, seq // c

    def specs(rev):
        ck = lambda s: _stream_chunk(s, rev, nc, nl)
        return [pl.BlockSpec((nb, c, w), lambda s: (0, ck(s), 0)),
                pl.BlockSpec((nb, c, w), lambda s: (0, ck(s), 1)),
                pl.BlockSpec((nb, c, w), lambda s: (0, ck(s), 2)),
                pl.BlockSpec((nb, c, LANES), lambda s: (0, ck(s), Z_LORA // LANES))]

    consts = [p['k_k'], p['k_a'], p['r_k'], p['w0'], p['w_up'], p['a0'], p['a_up'], p['bones64']]
    out = jax.ShapeDtypeStruct((nb, lt, w), F32)
    fwd = pl.BlockSpec((nb, c, w), lambda s: (0, _stream_chunk(s, False, nc, nl), 0))
    return pl.pallas_call(
        functools.partial(_rwkv_kernel, nb=nb),
        grid=(nc + nl,),
        in_specs=specs(False) + specs(True) + [_const(a.shape) for a in consts],
        out_specs=[fwd, pl.BlockSpec((nb, c, w), lambda s: (0, _stream_chunk(s, True, nc, nl), 0)), fwd],
        out_shape=[out, out, out],
        scratch_shapes=[pltpu.VMEM((2, nb, c, w), F32)],
        compiler_params=_params("arbitrary"),
        name="rwkv",
    )(zc, zc, zc, z, zc, zc, zc, z, *consts)


def _mlstm_kernel(qf, kf, vf, gcf, gtf, qr, kr, vr, gcr, gtr, gbr_ref, gbc_ref, sel_ref, hrow_ref, hf_ref, hr_ref,
                  c_scr, n_scr, m_scr, *, nb):
    c = CHUNK
    nh, dh, pw = ML_HEADS, ML_HEAD_DIM, ML_PAD

    @pl.when(pl.program_id(0) == 0)
    def _():
        c_scr[...] = jnp.zeros_like(c_scr)
        n_scr[...] = jnp.zeros_like(n_scr)
        m_scr[...] = jnp.zeros_like(m_scr)

    lane1 = lax.broadcasted_iota(jnp.int32, (1, LANES), 1)
    ti = lax.broadcasted_iota(jnp.int32, (c, c), 0)
    tj = lax.broadcasted_iota(jnp.int32, (c, c), 1)
    eye = (ti == tj).astype(BF16)
    hrow = hrow_ref[...]

    chains = []
    for d, (q_ref, k_ref, v_ref, gc_ref, gt_ref, h_ref) in enumerate(
            ((qf, kf, vf, gcf, gtf, hf_ref), (qr, kr, vr, gcr, gtr, hr_ref))):
        rev = d == 1
        tri = ((tj >= ti) if rev else (tj <= ti)).astype(BF16)
        valid_t = (ti >= tj) if rev else (ti <= tj)
        ig_lanes = jnp.logical_and(lane1 >= d * 8, lane1 < d * 8 + nh).astype(F32)
        fg_lanes = jnp.logical_and(lane1 >= d * 8 + nh, lane1 < d * 8 + 2 * nh).astype(F32)
        for b in range(nb):
            q = q_ref[b]
            q = (q * jax.nn.sigmoid(q)).astype(BF16)
            k = k_ref[b]
            k = (k * jax.nn.sigmoid(k) * (dh ** -0.5)).astype(BF16)
            v = v_ref[b].astype(BF16)
            gc = gc_ref[b] + gbr_ref[...]
            gt = gt_ref[b, 0] + gbc_ref[...]
            bc_cols = _mm_xl(tri, -_softplus(-gc))
            bc_rows = _mm_xr(-_softplus(-gt), tri, _NT)
            cm = _mm_xr(gc * ig_lanes - bc_cols * fg_lanes, sel_ref[d])
            n_prev = n_scr[d, b]
            chains.append(dict(d=d, b=b, rev=rev, valid_t=valid_t, q=q, k=k, v=v, gt=gt, bc_rows=bc_rows, cm=cm,
                               qt=_dot(q, eye, _TN), kt=_dot(k, eye, _TN),
                               nq=_dot((hrow * n_prev).astype(BF16), q, _NT),
                               n_prev=n_prev, h_ref=h_ref, m_all=m_scr[d, b]))
    for ch in chains:
        d, rev = ch['d'], ch['rev']
        heads = []
        for h in range(nh):
            hs = slice(h * pw, (h + 1) * pw)
            ci, cf = d * 8 + h, d * 8 + nh + h
            bcr = ch['bc_rows'][cf:cf + 1, :]
            icr = ch['gt'][ci:ci + 1, :]
            b_last = bcr[:, 0:1] if rev else bcr[:, c - 1:c]
            m_prev = ch['m_all'][:, h:h + 1]
            lt = jnp.where(ch['valid_t'], bcr + ch['cm'][:, h * c:(h + 1) * c], -jnp.inf)
            log_inter = bcr + m_prev
            m = jnp.maximum(log_inter, jnp.max(lt, axis=0, keepdims=True))
            inter = jnp.exp(log_inter - m)
            sm_t = _dot(ch['k'][:, hs], ch['q'][:, hs], _NT) * jnp.exp(lt - m)
            den = inter * ch['nq'][h:h + 1, :] + jnp.sum(sm_t, axis=0, keepdims=True)
            w1 = 1.0 / jnp.maximum(jnp.abs(den), jnp.exp(-m))
            log_w = b_last - bcr + icr
            m_new = jnp.maximum(b_last + m_prev, jnp.max(log_w, axis=-1, keepdims=True))
            heads.append(dict(hs=hs, sm_w=(sm_t * w1).astype(BF16), qt_w=(ch['qt'][hs, :] * (inter * w1)).astype(BF16),
                              m_new=m_new, wgt=jnp.exp(log_w - m_new), dec=jnp.exp(b_last + m_prev - m_new)))
        ch['heads'] = heads
    for ch in chains:
        d, b = ch['d'], ch['b']
        for h, hd in enumerate(ch['heads']):
            hs = hd['hs']
            lhs_t = jnp.concatenate([hd['sm_w'], hd['qt_w']], axis=0)
            rhs = jnp.concatenate([ch['v'][:, hs], c_scr[d, b, h].astype(BF16)], axis=0)
            ch['h_ref'][b, :, hs] = _dot(lhs_t, rhs, _TN)
    for ch in chains:
        d, b = ch['d'], ch['b']
        m_new_all = jnp.zeros((1, LANES), F32)
        wrows = jnp.zeros((SUBLANES, c), F32)
        dec_e = jnp.zeros((1, ML_PW), F32)
        sub = lax.broadcasted_iota(jnp.int32, (SUBLANES, 1), 0)
        for h, hd in enumerate(ch['heads']):
            hs = hd['hs']
            kt_w = (ch['kt'][hs, :] * hd['wgt']).astype(BF16)
            c_scr[d, b, h] = hd['dec'] * c_scr[d, b, h] + _dot(kt_w, ch['v'][:, hs])
            m_new_all = m_new_all + hd['m_new'] * (lane1 == h).astype(F32)
            wrows = wrows + hd['wgt'] * (sub == h).astype(F32)
            dec_e = dec_e + hd['dec'] * hrow[h:h + 1, :]
        n_add = jnp.sum(_dot(wrows.astype(BF16), ch['k']) * hrow, axis=0, keepdims=True)
        n_scr[d, b] = dec_e * ch['n_prev'] + n_add
        m_scr[d, b] = m_new_all


def _mlstm(zc_ml, z, glt, p, dims):
    nb, seq, ctx = dims
    lt = seq + ctx
    c = CHUNK
    nc, nl = ctx // c, seq // c
    pw = ML_PW

    def specs(rev):
        ck = lambda s: _stream_chunk(s, rev, nc, nl)
        return [pl.BlockSpec((nb, c, pw), lambda s: (0, ck(s), 0)),
                pl.BlockSpec((nb, c, pw), lambda s: (0, ck(s), 1)),
                pl.BlockSpec((nb, c, pw), lambda s: (0, ck(s), Z_MLV // pw)),
                pl.BlockSpec((nb, c, LANES), lambda s: (0, ck(s), Z_MLG // LANES)),
                pl.BlockSpec((nb, 1, 16, c), lambda s: (0, ck(s), 0, 0))]

    out = jax.ShapeDtypeStruct((nb, lt, pw), F32)
    consts = [p['gb_row'], p['gb_col'], p['ml_sel'], p['ml_hrow']]
    return pl.pallas_call(
        functools.partial(_mlstm_kernel, nb=nb),
        grid=(nc + nl,),
        in_specs=specs(False) + specs(True) + [_const(a.shape) for a in consts],
        out_specs=[pl.BlockSpec((nb, c, pw), lambda s: (0, _stream_chunk(s, False, nc, nl), 0)),
                   pl.BlockSpec((nb, c, pw), lambda s: (0, _stream_chunk(s, True, nc, nl), 0))],
        out_shape=[out, out],
        scratch_shapes=[pltpu.VMEM((2, nb, ML_HEADS, ML_PAD, ML_PAD), F32), pltpu.VMEM((2, nb, 1, pw), F32),
                        pltpu.VMEM((2, nb, 1, LANES), F32)],
        compiler_params=_params("arbitrary"),
        name="mlstm",
    )(zc_ml, zc_ml, z, z, glt, zc_ml, zc_ml, z, z, glt, *consts)


def _s5_kernel(u_ref, kt_ref, wb_ref, wc_ref, a_ref, y_ref, lhs_scr, st_scr, *, n, n_ctx):
    cs = S5_CHUNK
    hs = S5_HSTATE
    for i in range(cs):
        lhs_scr[:, i * S5_HALF:(i + 1) * S5_HALF] = u_ref[0, pl.ds(i, n, stride=cs), :].astype(BF16)
    lhs = lhs_scr[...]
    yacc = _dot(lhs, kt_ref[0])
    for d in range(2):
        st_scr[...] = _dot(lhs, wb_ref[d, 0])
        a = a_ref[d, 0]
        ar, ai = a[:, :hs], a[:, hs:]

        def body(t, x, d=d, ar=ar, ai=ai):
            if d == 1:
                idx = jnp.where(t < n_ctx, n_ctx - 1 - t, n + n_ctx - 1 - t)
            else:
                idx = t
            xr, xi = x
            row = st_scr[pl.ds(idx, 1), :]
            st_scr[pl.ds(idx, 1), :] = jnp.concatenate([xr, xi], axis=1)
            return (ar * xr - ai * xi + row[:, :hs], ar * xi + ai * xr + row[:, hs:])

        zero = jnp.zeros((1, hs), F32)
        lax.fori_loop(0, n, body, (zero, zero))
        yacc = yacc + _dot(st_scr[...].astype(BF16), wc_ref[d, 0])
    for j in range(cs):
        y_ref[0, pl.ds(j, n, stride=cs), :] = yacc[:, j * S5_HALF:(j + 1) * S5_HALF]


def _s5_branch(z, sp, dims):
    nb, seq, ctx = dims
    lt = seq + ctx
    n = lt // S5_CHUNK
    kw = S5_CHUNK * S5_HALF
    hs2 = 2 * S5_HSTATE
    kern = functools.partial(_s5_kernel, n=n, n_ctx=ctx // S5_CHUNK)
    once = pl.Buffered(1)

    def half(g):
        return pl.pallas_call(
            kern,
            grid=(nb,),
            in_specs=[pl.BlockSpec((1, lt, S5_HALF), lambda b: (b, 0, Z_S5 // S5_HALF + g)),
                      pl.BlockSpec((1, kw, kw), lambda b: (g, 0, 0), pipeline_mode=once),
                      pl.BlockSpec((2, 1, kw, hs2), lambda b: (0, g, 0, 0), pipeline_mode=once),
                      pl.BlockSpec((2, 1, hs2, kw), lambda b: (0, g, 0, 0), pipeline_mode=once),
                      pl.BlockSpec((2, 1, 1, hs2), lambda b: (0, g, 0, 0), pipeline_mode=once)],
            out_specs=pl.BlockSpec((1, lt, S5_HALF), lambda b: (b, 0, 0)),
            out_shape=jax.ShapeDtypeStruct((nb, lt, S5_HALF), F32),
            scratch_shapes=[pltpu.VMEM((n, kw), BF16), pltpu.VMEM((n, hs2), F32)],
            compiler_params=_params("parallel"),
            name="s5_scan",
        )(z, sp['ktoep'], sp['wb'], sp['wc'], sp['a_chunk'])

    return half(0), half(1)


def _s5_operators(p):
    cs = S5_CHUNK
    g, hch = S5_GROUPS, S5_GROUP_CH
    eye8 = jnp.eye(8, dtype=F32)
    kts, wbs, wcs, acs = 0.0, [], [], []
    for d in range(2):
        lr = jnp.minimum(p['s5_a_re'][d].astype(F32), -1e-4)
        li = p['s5_a_im'][d].astype(F32)
        dt = jnp.exp(p['s5_log_dt'][d].astype(F32))[:, None]
        tt = jnp.arange(cs + 1, dtype=F32)[:, None, None]
        mag = jnp.exp(lr * dt * tt)
        pr, pi = mag * jnp.cos(li * dt * tt), mag * jnp.sin(li * dt * tt)
        den = lr * lr + li * li
        qr = ((pr[1] - 1.0) * lr + pi[1] * li) / den
        qi = (pi[1] * lr - (pr[1] - 1.0) * li) / den
        b_re, b_im = p['s5_b_re'][d].astype(F32), p['s5_b_im'][d].astype(F32)
        bbr = qr[..., None] * b_re - qi[..., None] * b_im
        bbi = qr[..., None] * b_im + qi[..., None] * b_re
        cr, ci = p['s5_c_re'][d].astype(F32), p['s5_c_im'][d].astype(F32)
        abr = pr[:cs, :, :, None] * bbr[None] - pi[:cs, :, :, None] * bbi[None]
        abi = pr[:cs, :, :, None] * bbi[None] + pi[:cs, :, :, None] * bbr[None]
        ktau = jnp.einsum('ghn,tgni->tghi', cr, abr) - jnp.einsum('ghn,tgni->tghi', ci, abi)
        ii = jnp.arange(cs)[:, None]
        jj = jnp.arange(cs)[None, :]
        lag = (ii - jj) if d == 1 else (jj - ii)
        kts = kts + jnp.where((lag >= 0)[:, :, None, None, None], ktau[jnp.clip(lag, 0, cs - 1)], 0.0)
        e_in = jnp.arange(cs) if d == 1 else (cs - 1 - jnp.arange(cs))
        wbr, wbi = abr[e_in], abi[e_in]
        f_out = (cs - jnp.arange(cs)) if d == 1 else (jnp.arange(cs) + 1)
        wcr = cr[None] * pr[f_out][:, :, None, :] - ci[None] * pi[f_out][:, :, None, :]
        wci = cr[None] * pi[f_out][:, :, None, :] + ci[None] * pr[f_out][:, :, None, :]
        wb_h, wc_h, ac_h = [], [], []
        for half in range(2):
            gs = slice(half * 8, half * 8 + 8)
            wre = jnp.einsum('ignh,gk->ighkn', wbr[:, gs], eye8).reshape(cs * S5_HALF, S5_HSTATE)
            wim = jnp.einsum('ignh,gk->ighkn', wbi[:, gs], eye8).reshape(cs * S5_HALF, S5_HSTATE)
            wb_h.append(jnp.concatenate([wre, wim], axis=1))
            cre = jnp.einsum('jghn,gk->knjgh', wcr[:, gs], eye8).reshape(S5_HSTATE, cs * S5_HALF)
            cim = jnp.einsum('jghn,gk->knjgh', -wci[:, gs], eye8).reshape(S5_HSTATE, cs * S5_HALF)
            wc_h.append(jnp.concatenate([cre, cim], axis=0))
            ac_h.append(jnp.concatenate([pr[cs][gs].reshape(1, S5_HSTATE), pi[cs][gs].reshape(1, S5_HSTATE)], axis=1))
        wbs.append(jnp.stack(wb_h))
        wcs.append(jnp.stack(wc_h))
        acs.append(jnp.stack(ac_h))
    dsk = p['s5_d'].astype(F32).reshape(g, hch)
    eye_t = jnp.eye(cs, dtype=F32)[:, :, None, None, None]
    kts = kts + eye_t * (dsk[:, :, None] * jnp.eye(hch, dtype=F32)[None])[None, None]
    kt_h = []
    for half in range(2):
        gs = slice(half * 8, half * 8 + 8)
        kt_h.append(jnp.einsum('ijgoh,gk->ighjko', kts[:, :, gs], eye8).reshape(cs * S5_HALF, cs * S5_HALF))
    return dict(ktoep=jnp.stack(kt_h).astype(BF16), wb=jnp.stack(wbs).astype(BF16),
                wc=jnp.stack(wcs).astype(BF16), a_chunk=jnp.stack(acs))


def _merge_kernel(x_ref, m_ref, yf_ref, yr_ref, bonus_ref, gdn_ref, hf_ref, hr_ref, o_ref, s5a_ref, s5b_ref,
                  br0_ref, br1_ref, br2_ref,
                  gup_ref, gng_ref, gnb_ref, bones64_ref, glw_ref, glb_ref, mlg_ref, bones_ml_ref, mlmask_ref,
                  brb_ref, uprw_ref, ups5_ref, upml_ref, wout_ref, lng_ref, lnb_ref, out_ref, *, nb, tr, ctx_tiles):
    d = D_MODEL
    is_ctx = pl.program_id(0) < ctx_tiles
    rows = nb * tr
    flat = lambda ref: ref[...].reshape(rows, ref.shape[-1])
    ys = flat(yf_ref) + flat(yr_ref)
    b64 = bones64_ref[...]
    yc = ys - _mm_xr(ys, b64) * (1.0 / RW_HEAD_DIM)
    var = _mm_xr(yc * yc, b64) * (1.0 / RW_HEAD_DIM)
    yn = yc * lax.rsqrt(var + RW_GN_EPS) * gng_ref[...] + gnb_ref[...]
    rw_y = (yn + flat(bonus_ref)) * _mm(jax.nn.sigmoid(flat(gdn_ref)), gup_ref[...])
    s5 = jax.nn.gelu(jnp.concatenate([flat(s5a_ref), flat(s5b_ref)], axis=1))
    s5_y = s5 * jax.nn.sigmoid(_mm(s5, glw_ref[...]) + glb_ref[...])
    hg = jax.nn.sigmoid(flat(o_ref)) * (flat(hf_ref) + flat(hr_ref))
    bml = bones_ml_ref[...]
    hc = (hg - _mm_xr(hg, bml) * (1.0 / ML_HEAD_DIM)) * mlmask_ref[...]
    hv = _mm_xr(hc * hc, bml) * (1.0 / ML_HEAD_DIM)
    ml_y = hc * lax.rsqrt(hv + ML_EPS) * mlg_ref[...]
    brb = brb_ref[...]
    y = (jax.nn.sigmoid(flat(br0_ref) + brb[:, 0:d]) * _mm(rw_y, uprw_ref[...])
         + jax.nn.sigmoid(flat(br1_ref) + brb[:, d:2 * d]) * _mm(s5_y, ups5_ref[...])
         + jax.nn.sigmoid(flat(br2_ref) + brb[:, 2 * d:3 * d]) * _mm(ml_y, upml_ref[...]))
    out = _mm(y, wout_ref[...])
    for b in range(nb):
        gate = _mod_vec(m_ref, is_ctx, b, nb, 5)
        out_ref[b] = _layer_norm(DEEPNORM_ALPHA * x_ref[b] + gate * out[b * tr:(b + 1) * tr, :],
                                 lng_ref[...], lnb_ref[...])


def _merge(xs, mods, yf, yr, bonus, hf, hr, s5y, z, p, dims):
    nb, seq, ctx = dims
    lt = seq + ctx
    d = D_MODEL
    tr = _pick_tile((128, 64), seq, ctx)
    row = lambda width, col=0: pl.BlockSpec((nb, tr, width), lambda i: (0, i, col))
    consts = [p['g_up'], p['gn_g'], p['gn_b'], p['bones64'], p['glu_w'], p['glu_b'], p['ml_norm_g'], p['bones_ml'],
              p['ml_mask'], p['br_b'], p['up_rw'], p['up_s5'], p['up_ml'], p['w_out'], p['ln_g1'], p['ln_b1']]
    kern = functools.partial(_merge_kernel, nb=nb, tr=tr, ctx_tiles=ctx // tr)
    return pl.pallas_call(
        kern,
        grid=(lt // tr,),
        in_specs=[row(d), _const(mods.shape),
                  row(RW_WIDTH), row(RW_WIDTH), row(RW_WIDTH), row(RW_GATE_LORA, Z_GDN // RW_GATE_LORA),
                  row(ML_PW), row(ML_PW), row(ML_PW, Z_MLO // ML_PW), row(S5_HALF), row(S5_HALF),
                  row(d, Z_BR // d), row(d, Z_BR // d + 1), row(d, Z_BR // d + 2)] + [_const(a.shape) for a in consts],
        out_specs=row(d),
        out_shape=jax.ShapeDtypeStruct((nb, lt, d), F32),
        compiler_params=_params("parallel"),
        name="merge",
    )(xs, mods, yf, yr, bonus, z, hf, hr, z, s5y[0], s5y[1], z, z, z, *consts)


def _pad_heads(w, axis):
    shape = w.shape
    w = w.reshape(shape[:axis] + (ML_HEADS, ML_HEAD_DIM) + shape[axis + 1:])
    pad = [(0, 0)] * w.ndim
    pad[axis + 1] = (0, ML_PAD - ML_HEAD_DIM)
    return jnp.pad(w, pad).reshape(shape[:axis] + (ML_PW,) + shape[axis + 1:])


def _permute_w_in(w):
    old = {}
    start = 0
    for name, size in (('rw_conv', 3 * RW_WIDTH), ('ml_q', ML_WIDTH), ('ml_k', ML_WIDTH), ('ml_v', ML_WIDTH),
                       ('ml_o', ML_WIDTH), ('ml_gl', 4 * ML_HEADS), ('s5_u', S5_WIDTH), ('w_dn', RW_DECAY_LORA),
                       ('a_dn', RW_AAA_LORA), ('g_dn', RW_GATE_LORA), ('br_gl', N_BRANCH * D_MODEL)):
        old[name] = w[:, start:start + size]
        start += size
    zeros = lambda n: jnp.zeros((w.shape[0], n), w.dtype)
    out = jnp.concatenate([old['rw_conv'], old['w_dn'], old['a_dn'], old['g_dn'], old['ml_gl'],
                           zeros(LANES - 4 * ML_HEADS), old['s5_u'], zeros(Z_MLV - Z_S5 - S5_WIDTH),
                           _pad_heads(old['ml_v'], 1), _pad_heads(old['ml_o'], 1), _pad_heads(old['ml_q'], 1),
                           _pad_heads(old['ml_k'], 1), old['br_gl']], axis=1)
    assert out.shape[1] == Z_WIDTH
    return out


def _ml_gate_select():
    sel = np.zeros((2, LANES, ML_HEADS * CHUNK), np.float32)
    for d in range(2):
        for h in range(ML_HEADS):
            sel[d, d * 8 + h, h * CHUNK:(h + 1) * CHUNK] = 1.0
            sel[d, d * 8 + ML_HEADS + h, h * CHUNK:(h + 1) * CHUNK] = 1.0
    return sel


def _ml_head_rows():
    rows = np.zeros((SUBLANES, ML_PW), np.float32)
    for h in range(ML_HEADS):
        rows[h, h * ML_PAD:(h + 1) * ML_PAD] = 1.0
    return rows


def _block_ones(width, group):
    g = np.arange(width) // group
    return (g[:, None] == g[None, :]).astype(np.float32)


def _layer_params(i, a):
    w = RW_WIDTH
    pad_lo = jnp.zeros((2, RW_AAA_LORA, w), F32)
    gb = a['ml_gate_b'][i].reshape(16).astype(F32)
    ml_real = (np.arange(ML_PW) % ML_PAD < ML_HEAD_DIM).astype(np.float32)
    return dict(
        k_k=a['rw_k_k'][i].reshape(1, w), k_a=a['rw_k_a'][i].reshape(1, w), r_k=a['rw_r_k'][i].reshape(1, w),
        w0=a['rw_w0'][i].reshape(2, 1, w), a0=a['rw_a0'][i].reshape(2, 1, w),
        w_up=jnp.concatenate([a['rw_w_up'][i], pad_lo], axis=1).astype(BF16),
        a_up=jnp.concatenate([pad_lo, a['rw_a_up'][i]], axis=1).astype(BF16),
        bones64=jnp.asarray(_block_ones(w, RW_HEAD_DIM), BF16),
        bones_ml=jnp.asarray(_block_ones(ML_PW, ML_PAD), BF16),
        ml_mask=jnp.asarray(ml_real.reshape(1, ML_PW)),
        ml_sel=jnp.asarray(_ml_gate_select(), BF16), ml_hrow=jnp.asarray(_ml_head_rows()),
        gb_row=jnp.concatenate([gb, jnp.zeros((LANES - 16,), F32)]).reshape(1, LANES), gb_col=gb.reshape(16, 1),
        g_up=a['rw_g_up'][i].astype(BF16), gn_g=a['rw_gn_g'][i].reshape(1, w), gn_b=a['rw_gn_b'][i].reshape(1, w),
        glu_w=a['s5_glu_w'][i].astype(BF16), glu_b=a['s5_glu_b'][i].reshape(1, S5_WIDTH),
        ml_norm_g=_pad_heads(a['ml_norm_g'][i].reshape(1, ML_WIDTH), 1),
        br_b=a['br_gate_b'][i].reshape(1, N_BRANCH * D_MODEL),
        up_rw=a['up_rw'][i].astype(BF16), up_s5=a['up_s5'][i].astype(BF16),
        up_ml=_pad_heads(a['up_ml'][i], 0).astype(BF16),
        w_out=a['w_out'][i].astype(BF16),
        ln_g1=a['ln_g'][i, 1].reshape(1, D_MODEL), ln_b1=a['ln_b'][i, 1].reshape(1, D_MODEL),
        conv_rw=a['conv_w'][i][:, :, :3 * RW_WIDTH],
        conv_ml=jnp.concatenate([_pad_heads(a['conv_w'][i][:, :, 3 * RW_WIDTH:3 * RW_WIDTH + ML_WIDTH], 2),
                                 _pad_heads(a['conv_w'][i][:, :, 3 * RW_WIDTH + ML_WIDTH:], 2)], axis=2),
    )


def kernel(x, c, ctx, c_ctx, ada_w, ada_b, ln_g, ln_b, ffn_w_gate, ffn_w_up, ffn_w_down, w_in, conv_w, rw_w0, rw_w_up, rw_a0, rw_a_up, rw_g_up, rw_k_k, rw_k_a, rw_r_k, rw_gn_g, rw_gn_b, s5_a_re, s5_a_im, s5_log_dt, s5_b_re, s5_b_im, s5_c_re, s5_c_im, s5_d, s5_glu_w, s5_glu_b, ml_gate_b, ml_norm_g, up_rw, up_s5, up_ml, br_gate_b, w_out):
    a = dict(ln_g=ln_g, ln_b=ln_b, conv_w=conv_w, rw_w0=rw_w0, rw_w_up=rw_w_up, rw_a0=rw_a0, rw_a_up=rw_a_up,
             rw_g_up=rw_g_up, rw_k_k=rw_k_k, rw_k_a=rw_k_a, rw_r_k=rw_r_k, rw_gn_g=rw_gn_g, rw_gn_b=rw_gn_b,
             s5_glu_w=s5_glu_w, s5_glu_b=s5_glu_b, ml_gate_b=ml_gate_b, ml_norm_g=ml_norm_g, up_rw=up_rw,
             up_s5=up_s5, up_ml=up_ml, br_gate_b=br_gate_b, w_out=w_out)
    nb, seq, d = x.shape
    nctx = ctx.shape[1]
    dims = (nb, seq, nctx)
    lt = seq + nctx
    depth = ada_w.shape[0]
    rows = -(-(nb + 1) // SUBLANES) * SUBLANES
    cv = jnp.concatenate([c, c_ctx[None, :], jnp.zeros((rows - nb - 1, d), F32)], axis=0)
    mods = _modulation(cv, ada_w, ada_b)
    ffn_bf16 = (ffn_w_gate.astype(BF16), ffn_w_up.astype(BF16), ffn_w_down.astype(BF16))
    xs = None
    for i in range(depth):
        m = mods[i]
        p = _layer_params(i, a)
        sp = _s5_operators(dict(s5_a_re=s5_a_re[i], s5_a_im=s5_a_im[i], s5_log_dt=s5_log_dt[i], s5_b_re=s5_b_re[i],
                                s5_b_im=s5_b_im[i], s5_c_re=s5_c_re[i], s5_c_im=s5_c_im[i], s5_d=s5_d[i]))
        w_in_p = _permute_w_in(w_in[i].astype(BF16))
        ffn_w = lambda j: (*ffn_bf16, (i, j))
        tr = _ffn_tile(dims)
        ct, st = nctx // tr, seq // tr
        ln0 = (ln_g[i, 0], ln_b[i, 0])
        if i == 0:
            xs = _ffn_half(x, m, 0, *ffn_w(0), *ln0, dims, src_tile0=0, n_tiles=ct + st, n_ctx_tiles=ct,
                           dst_rows=lt, ctx_src=ctx)
        else:
            xs = _ffn_half(xs, m, 0, *ffn_w(0), *ln0, dims, src_tile0=0, n_tiles=ct + st, n_ctx_tiles=ct,
                           dst_rows=lt)
        gl0 = 3 * RW_WIDTH + 4 * ML_WIDTH
        w_gate_t = jnp.pad(w_in[i][:, gl0:gl0 + 4 * ML_HEADS].T.astype(BF16), ((0, LANES - 4 * ML_HEADS), (0, 0)))
        z, glt = _in_proj(xs, m, w_in_p, w_gate_t, dims)
        zc_rw = _short_conv(z, p['conv_rw'], Z_RWC, dims)
        zc_ml = _short_conv(z, p['conv_ml'], Z_MLC, dims)
        yf, yr, bonus = _rwkv(zc_rw, z, p, dims)
        hf, hr = _mlstm(zc_ml, z, glt, p, dims)
        s5y = _s5_branch(z, sp, dims)
        xs = _merge(xs, m, yf, yr, bonus, hf, hr, s5y, z, p, dims)
        if i == depth - 1:
            xs = _ffn_half(xs, m, 6, *ffn_w(1), ln_g[i, 2], ln_b[i, 2], dims, src_tile0=ct, n_tiles=st,
                           n_ctx_tiles=0, dst_rows=seq)
        else:
            xs = _ffn_half(xs, m, 6, *ffn_w(1), ln_g[i, 2], ln_b[i, 2], dims, src_tile0=0, n_tiles=ct + st,
                           n_ctx_tiles=ct, dst_rows=lt)
    return xs
```

```python
import functools

import numpy as np
import jax
import jax.numpy as jnp
from jax import lax
from jax.experimental import pallas as pl
from jax.experimental.pallas import tpu as pltpu

F32 = jnp.float32
BF16 = jnp.bfloat16

D_MODEL = 1024
DEPTH = 2
GRID_W = 64

RW_HEADS = 6
RW_HEAD_DIM = 64
RW_WIDTH = 384
RW_DECAY_LORA = 64
RW_AAA_LORA = 64
RW_GATE_LORA = 128
RW_GN_EPS = 64e-5

S5_GROUPS = 16
S5_GROUP_CH = 16
S5_WIDTH = 256
S5_STATE = 64

ML_HEADS = 4
ML_HEAD_DIM = 96
ML_WIDTH = 384
ML_EPS = 1e-5

D_FF = 2816
N_BRANCH = 3
N_MOD = 9
LN_EPS = 1e-5
DEEPNORM_ALPHA = (2.0 * DEPTH) ** 0.25

LANES = 128
SUBLANES = 8
MXU_COLS = 256

CHUNK = 64
S5_CHUNK = 8
S5_HALF = 128
S5_HSTATE = 512
ML_PAD = LANES
ML_PW = ML_HEADS * ML_PAD

Z_RWC = 0
Z_LORA = Z_RWC + 3 * RW_WIDTH
Z_GDN = Z_LORA + LANES
Z_MLG = Z_GDN + RW_GATE_LORA
Z_S5 = Z_MLG + LANES
Z_MLV = 2048
Z_MLO = Z_MLV + ML_PW
Z_MLC = Z_MLO + ML_PW
Z_BR = Z_MLC + 2 * ML_PW
Z_WIDTH = Z_BR + N_BRANCH * D_MODEL

VMEM_LIMIT = 58 * 1024 * 1024

_NN = (((1,), (0,)), ((), ()))
_NT = (((1,), (1,)), ((), ()))
_TN = (((0,), (0,)), ((), ()))


def _dot(a, b, dims=_NN):
    return lax.dot_general(a, b, dims, preferred_element_type=F32)


def _mm(a, b, dims=_NN):
    return _dot(a.astype(BF16), b.astype(BF16), dims)


def _split(a, n):
    parts, r = [], a
    for _ in range(n):
        p = r.astype(BF16)
        parts.append(p)
        r = r - p.astype(F32)
    return parts


def _mm_xr(a, b_exact, dims=_NN):
    hi, lo = _split(a, 2)
    m = a.shape[0]
    r = _dot(jnp.concatenate([hi, lo], axis=0), b_exact, dims)
    return r[:m] + r[m:]


def _mm_xl(a_exact, b, dims=_NN):
    hi, lo = _split(b, 2)
    return _dot(a_exact, hi, dims) + _dot(a_exact, lo, dims)


def _mm3(a, b, dims=_NN):
    ah, al = _split(a, 2)
    bh, bl = _split(b, 2)
    return _dot(ah, bh, dims) + (_dot(al, bh, dims) + _dot(ah, bl, dims))


def _softplus(x):
    return jnp.maximum(x, 0.0) + jnp.log1p(jnp.exp(-jnp.abs(x)))


def _layer_norm(y, g, b):
    yc = y - jnp.mean(y, axis=-1, keepdims=True)
    return yc * lax.rsqrt(jnp.mean(yc * yc, axis=-1, keepdims=True) + LN_EPS) * g + b


def _pick_tile(cands, *lens):
    for t in cands:
        if all(n % t == 0 for n in lens):
            return t
    raise ValueError(f"no tile in {cands} divides {lens}")


def _params(*sem):
    return pltpu.CompilerParams(dimension_semantics=sem, vmem_limit_bytes=VMEM_LIMIT)


def _const(shape):
    zeros = (0,) * len(shape)
    return pl.BlockSpec(shape, lambda *_: zeros, pipeline_mode=pl.Buffered(1))


def _mod_vec(m_ref, is_ctx, b, nb, k):
    row = jnp.where(is_ctx, nb, b)
    return m_ref[pl.ds(row, 1), k * D_MODEL:(k + 1) * D_MODEL]


def _mod_kernel(cv_ref, w_ref, b_ref, o_ref):
    s = cv_ref[...]
    s = s * jax.nn.sigmoid(s)
    o_ref[0] = _mm3(s, w_ref[0]) + b_ref[0]


def _modulation(cv, ada_w, ada_b):
    depth, d, nd = ada_w.shape
    rows = cv.shape[0]
    return pl.pallas_call(
        _mod_kernel,
        grid=(depth, nd // d),
        in_specs=[pl.BlockSpec((rows, d), lambda l, j: (0, 0)),
                  pl.BlockSpec((1, d, d), lambda l, j: (l, 0, j)),
                  pl.BlockSpec((1, 1, d), lambda l, j: (l, 0, j))],
        out_specs=pl.BlockSpec((1, rows, d), lambda l, j: (l, 0, j)),
        out_shape=jax.ShapeDtypeStruct((depth, rows, nd), F32),
        compiler_params=_params("parallel", "parallel"),
        name="modulation",
    )(cv, ada_w, ada_b.reshape(depth, 1, nd))


def _ffn_kernel(x_ref, m_ref, wg_ref, wu_ref, wd_ref, g_ref, b_ref, *rest, k0, nb, tr, n_ctx_tiles):
    o_ref, u_scr, h_scr = rest[-3:]
    is_ctx = pl.program_id(0) < n_ctx_tiles
    x_in = (lambda b: jnp.where(is_ctx, rest[0][b], x_ref[b])) if len(rest) == 4 else (lambda b: x_ref[b])
    for b in range(nb):
        shift = _mod_vec(m_ref, is_ctx, b, nb, k0)
        scale = _mod_vec(m_ref, is_ctx, b, nb, k0 + 1)
        u_scr[b * tr:(b + 1) * tr, :] = (x_in(b) * (1.0 + scale) + shift).astype(BF16)
    u = u_scr[...]
    for f in range(D_FF // MXU_COLS):
        cols = slice(f * MXU_COLS, (f + 1) * MXU_COLS)
        g = _dot(u, wg_ref[0, 0, :, cols])
        up = _dot(u, wu_ref[0, 0, :, cols])
        h_scr[:, cols] = (g * jax.nn.sigmoid(g) * up).astype(BF16)
    acc = _dot(h_scr[...], wd_ref[0, 0])
    for b in range(nb):
        gate = _mod_vec(m_ref, is_ctx, b, nb, k0 + 2)
        y = DEEPNORM_ALPHA * x_in(b) + (0.5 * gate) * acc[b * tr:(b + 1) * tr, :]
        o_ref[b] = _layer_norm(y, g_ref[...], b_ref[...])


def _ffn_half(src, mods, k0, wg, wu, wd, wi, g, b, dims, *, src_tile0, n_tiles, n_ctx_tiles, dst_rows,
              ctx_src=None):
    nb, seq, ctx = dims
    d = D_MODEL
    tr = _ffn_tile(dims)
    kern = functools.partial(_ffn_kernel, k0=k0, nb=nb, tr=tr, n_ctx_tiles=n_ctx_tiles)
    weight = lambda r, c: pl.BlockSpec((1, 1, r, c), lambda i: (wi[0], wi[1], 0, 0), pipeline_mode=pl.Buffered(1))
    in_specs = [pl.BlockSpec((nb, tr, d), lambda i: (0, i + src_tile0, 0)),
                _const(mods.shape), weight(d, D_FF), weight(d, D_FF), weight(D_FF, d),
                _const((1, d)), _const((1, d))]
    args = [src, mods, wg, wu, wd, g.reshape(1, d), b.reshape(1, d)]
    if ctx_src is not None:
        assert n_ctx_tiles == 1 and src_tile0 == 0
        in_specs[0] = pl.BlockSpec((nb, tr, d), lambda i: (0, jnp.maximum(i - 1, 0), 0))
        in_specs.append(_const((nb, tr, d)))
        args.append(ctx_src)
    return pl.pallas_call(
        kern,
        grid=(n_tiles,),
        in_specs=in_specs,
        out_specs=pl.BlockSpec((nb, tr, d), lambda i: (0, i, 0)),
        out_shape=jax.ShapeDtypeStruct((nb, dst_rows, d), F32),
        scratch_shapes=[pltpu.VMEM((nb * tr, d), BF16), pltpu.VMEM((nb * tr, D_FF), BF16)],
        compiler_params=_params("parallel"),
        name="ffn_half",
    )(*args)


def _ffn_tile(dims):
    return _pick_tile((256, 128, 64), dims[1], dims[2])


def _inproj_kernel(x_ref, m_ref, w_ref, wgt_ref, o_ref, gt_ref, u_scr, *, nb, tr, ctx_tiles):
    is_ctx = pl.program_id(0) < ctx_tiles

    @pl.when(pl.program_id(1) == 0)
    def _():
        for b in range(nb):
            shift = _mod_vec(m_ref, is_ctx, b, nb, 3)
            scale = _mod_vec(m_ref, is_ctx, b, nb, 4)
            u_scr[b * tr:(b + 1) * tr, :] = (x_ref[b] * (1.0 + scale) + shift).astype(BF16)
        gt = _dot(wgt_ref[...], u_scr[...], _NT)
        for b in range(nb):
            for q in range(tr // CHUNK):
                t0 = b * tr + q * CHUNK
                gt_ref[b, q] = gt[0:4 * ML_HEADS, t0:t0 + CHUNK]

    res = _dot(u_scr[...], w_ref[...])
    for b in range(nb):
        o_ref[b] = res[b * tr:(b + 1) * tr, :]


def _in_proj(xs, mods, w_in_p, w_gate_t, dims):
    nb, seq, ctx = dims
    d = D_MODEL
    lt = seq + ctx
    tr = _pick_tile((256, 128, 64), seq, ctx)
    tn = 7 * MXU_COLS
    ng = 4 * ML_HEADS
    kern = functools.partial(_inproj_kernel, nb=nb, tr=tr, ctx_tiles=ctx // tr)
    return pl.pallas_call(
        kern,
        grid=(lt // tr, Z_WIDTH // tn),
        in_specs=[pl.BlockSpec((nb, tr, d), lambda i, j: (0, i, 0)),
                  pl.BlockSpec(mods.shape, lambda i, j: (0, 0)),
                  pl.BlockSpec((d, tn), lambda i, j: (0, j)),
                  pl.BlockSpec(w_gate_t.shape, lambda i, j: (0, 0))],
        out_specs=[pl.BlockSpec((nb, tr, tn), lambda i, j: (0, i, j)),
                   pl.BlockSpec((nb, tr // CHUNK, ng, CHUNK), lambda i, j: (0, i, 0, 0))],
        out_shape=[jax.ShapeDtypeStruct((nb, lt, Z_WIDTH), F32),
                   jax.ShapeDtypeStruct((nb, lt // CHUNK, ng, CHUNK), F32)],
        scratch_shapes=[pltpu.VMEM((nb * tr, d), BF16)],
        compiler_params=_params("parallel", "arbitrary"),
        name="in_proj",
    )(xs, mods, w_in_p, w_gate_t)


def _conv_kernel(prev_ref, cur_ref, next_ref, w_ref, o_ref, *, tmc, ctx, lt):
    row0 = pl.program_id(1) * tmc
    is_lat = row0 >= ctx
    has_prev = row0 > ctx
    has_next = jnp.logical_and(is_lat, row0 + tmc < lt)
    pos = lax.broadcasted_iota(jnp.int32, (tmc, 1), 0)
    col = lax.rem(pos, GRID_W)
    keep = lambda drop: jnp.where(drop, 0.0, 1.0)
    m_left = jnp.where(is_lat, keep(col == 0), keep(jnp.logical_and(pos == 0, row0 == 0)))
    m_right = jnp.where(is_lat, keep(col == GRID_W - 1),
                        keep(jnp.logical_and(pos == tmc - 1, row0 + tmc == ctx)))
    cur = cur_ref[0]
    prev = jnp.where(has_prev, prev_ref[0], 0.0)
    nxt = jnp.where(has_next, next_ref[0], 0.0)
    if tmc > GRID_W:
        up = jnp.concatenate([prev, cur[:tmc - GRID_W]], axis=0)
        dn = jnp.concatenate([cur[GRID_W:], nxt], axis=0)
    else:
        up, dn = prev, nxt
    lat_f = jnp.where(is_lat, 1.0, 0.0)
    w = w_ref[...]
    w_up = w[0] * lat_f
    w_dn = w[2] * lat_f

    def tap(j):
        return up * w_up[j:j + 1, :] + cur * w[1, j:j + 1, :] + dn * w_dn[j:j + 1, :]

    o_ref[0] = m_left * pltpu.roll(tap(0), 1, 0) + tap(1) + m_right * pltpu.roll(tap(2), tmc - 1, 0)


def _short_conv(z, conv_w, col0, dims):
    nb, seq, ctx = dims
    lt = seq + ctx
    cw = conv_w.shape[-1]
    tmc = _pick_tile((256, 128, 64), seq, ctx)
    assert tmc == ctx, "the context sequence conv expects one tile per context segment"
    hb = tmc // GRID_W
    nblk = lt // GRID_W
    cb = col0 // cw
    kern = functools.partial(_conv_kernel, tmc=tmc, ctx=ctx, lt=lt)
    return pl.pallas_call(
        kern,
        grid=(nb, lt // tmc),
        in_specs=[pl.BlockSpec((1, GRID_W, cw), lambda b, i: (b, jnp.maximum(i * hb - 1, 0), cb)),
                  pl.BlockSpec((1, tmc, cw), lambda b, i: (b, i, cb)),
                  pl.BlockSpec((1, GRID_W, cw), lambda b, i: (b, jnp.minimum(i * hb + hb, nblk - 1), cb)),
                  pl.BlockSpec((3, 3, cw), lambda b, i: (0, 0, 0))],
        out_specs=pl.BlockSpec((1, tmc, cw), lambda b, i: (b, i, 0)),
        out_shape=jax.ShapeDtypeStruct((nb, lt, cw), F32),
        compiler_params=_params("parallel", "parallel"),
        name="short_conv",
    )(z, z, z, conv_w)


def _stream_chunk(s, rev, nc, nl):
    if not rev:
        return s
    return jnp.where(s < nc, nc - 1 - s, 2 * nc + nl - 1 - s)


def _pair_masks():
    lane = lax.broadcasted_iota(jnp.int32, (1, LANES), 1)
    m0 = (lane < RW_HEAD_DIM).astype(F32)
    return m0, 1.0 - m0


def _make_bd(m0, m1):
    m0h, m1h = m0.astype(BF16), m1.astype(BF16)

    def bd(y):
        y = y.astype(BF16)
        n = y.shape[1] // LANES
        a0 = jnp.concatenate([m0h] * n, axis=1) if n > 1 else m0h
        a1 = jnp.concatenate([m1h] * n, axis=1) if n > 1 else m1h
        return jnp.concatenate([y * a0, y * a1], axis=0)

    return bd


def _rwkv_kernel(rf, kf, vf, lof, rr, kr, vr, lor, kk_ref, ka_ref, rk_ref, w0_ref, wup_ref, a0_ref, aup_ref,
                 bones_ref, yf_ref, yr_ref, bonus_ref, st_scr, *, nb):
    c = CHUNK
    bones = bones_ref[...]
    m0, m1 = _pair_masks()
    bd = _make_bd(m0, m1)

    @pl.when(pl.program_id(0) == 0)
    def _():
        st_scr[...] = jnp.zeros_like(st_scr)

    ti = lax.broadcasted_iota(jnp.int32, (c, c), 0)
    tj = lax.broadcasted_iota(jnp.int32, (c, c), 1)
    pt_i = lax.broadcasted_iota(jnp.int32, (c, LANES), 0)
    pt_j = lax.rem(lax.broadcasted_iota(jnp.int32, (c, LANES), 1), c)
    eye_p = (pt_i == pt_j).astype(F32)

    chains = []
    for d, (r_ref, k_ref, v_ref, lo_ref, y_ref) in enumerate(((rf, kf, vf, lof, yf_ref), (rr, kr, vr, lor, yr_ref))):
        rev = d == 1
        tri = ((tj >= ti) if rev else (tj <= ti)).astype(BF16)
        strict = ((pt_j > pt_i) if rev else (pt_j < pt_i)).astype(F32)
        incl = strict + eye_p
        for b in range(nb):
            r = r_ref[b]
            k = k_ref[b]
            v = v_ref[b]
            lo = lo_ref[b]
            kk = k * kk_ref[...]
            kk = kk * lax.rsqrt(jnp.maximum(_mm_xr(kk * kk, bones), 1e-24))
            tw = jnp.tanh(lo)
            a_of = lambda dd: jax.nn.sigmoid(a0_ref[dd] + _mm(lo, aup_ref[dd]))
            kd_of = lambda a: k * (1.0 + (a - 1.0) * ka_ref[...])
            wl = w0_ref[d] + _mm(tw, wup_ref[d])
            lw = -jnp.exp(-_softplus(-wl) - 0.5)
            a = a_of(d)
            kd = kd_of(a)
            bb = kk * a
            if d == 0:
                bonus_ref[b] = _mm_xr(r * (kd + kd_of(a_of(1))) * rk_ref[...], bones) * v
            cs = _mm_xl(tri, lw)
            csp = cs - lw
            tot = cs[0:1, :] if rev else cs[c - 1:c, :]
            mid = 0.5 * tot
            e_dn = jnp.exp(mid - cs)
            a_hat = kk * jnp.exp(csp - mid)
            r_hat = r * jnp.exp(cs - mid)
            b_hat = bb * e_dn
            k_hat = kd * e_dn
            a_st = kk * jnp.exp(csp)
            r_st = r * jnp.exp(cs)
            e_end = jnp.exp(tot - cs)
            b_end = bb * e_end
            k_end = kd * e_end
            w_c = jnp.exp(tot)
            for p in range(RW_HEADS // 2):
                sl = slice(p * LANES, (p + 1) * LANES)
                chains.append(dict(d=d, b=b, sl=sl, y_ref=y_ref, strict=strict, incl=incl, vp=v[:, sl],
                                   a_hat=a_hat[:, sl], r_hat=r_hat[:, sl], b_hat=b_hat[:, sl], k_hat=k_hat[:, sl],
                                   a_st=a_st[:, sl], r_st=r_st[:, sl], b_end=b_end[:, sl], k_end=k_end[:, sl],
                                   w_c=w_c[:, sl]))

    for ch in chains:
        lhs = jnp.concatenate([ch['a_hat'], ch['r_hat']], axis=0).astype(BF16)
        rhs_t = jnp.concatenate([bd(ch['b_hat']), bd(ch['k_hat'])], axis=0)
        g4 = _dot(lhs, rhs_t, _NT)
        ch['lp'] = g4[0:c, 0:LANES] * ch['strict']
        ch['t'] = eye_p - ch['lp']
        ch['m_rb'] = (g4[c:, 0:LANES] * ch['incl']).astype(BF16)
        ch['lak_mrk'] = jnp.concatenate([g4[0:c, LANES:] * ch['strict'], g4[c:, LANES:] * ch['incl']],
                                        axis=0).astype(BF16)
    for ch in chains:
        ch['lp'] = _dot(ch['lp'].astype(BF16), bd(ch['lp']))
    for it in range(5):
        for ch in chains:
            if it < 4:
                res = _dot(jnp.concatenate([ch['t'], ch['lp']], axis=0).astype(BF16), bd(ch['lp']))
                ch['t'] = ch['t'] + res[0:c]
                ch['lp'] = res[c:]
            else:
                ch['t'] = ch['t'] + _dot(ch['t'].astype(BF16), bd(ch['lp']))
    for ch in chains:
        xv = _dot(ch['lak_mrk'], bd(ch['vp']))
        ch['x0'] = xv[0:c]
        ch['mrk_v'] = xv[c:]
    for ch in chains:
        ch['ua'] = _dot(ch['t'].astype(BF16), bd(jnp.concatenate([ch['x0'], ch['a_st']], axis=1)))
    for ch in chains:
        mu = _dot(ch['m_rb'], bd(ch['ua']))
        ch['y1'] = ch['mrk_v'] - mu[:, 0:LANES]
        ch['rq'] = ch['r_st'] - mu[:, LANES:]
    for ch in chains:
        ua = ch['ua']
        rhs = jnp.concatenate([jnp.concatenate([ch['vp'], jnp.zeros_like(ch['vp'])], axis=1),
                               jnp.concatenate([-ua[:, 0:LANES], ua[:, LANES:]], axis=1)], axis=0)
        hg = _mm(jnp.concatenate([ch['k_end'], ch['b_end']], axis=0), rhs, _TN)
        ch['pt'] = eye_p * ch['w_c'] - (hg[0:c, LANES:] * m0 + hg[c:, LANES:] * m1)
        ch['qt'] = hg[0:c, 0:LANES] * m0 + hg[c:, 0:LANES] * m1
    for ch in chains:
        d, b, sl = ch['d'], ch['b'], ch['sl']
        res = _dot(jnp.concatenate([ch['rq'], ch['pt']], axis=0).astype(BF16), bd(st_scr[d, b, :, sl]))
        ch['y_ref'][b, :, sl] = ch['y1'] + res[0:c]
        st_scr[d, b, :, sl] = res[c:] + ch['qt']


def _rwkv(zc, z, p, dims):
    nb, seq, ctx = dims
    lt = seq + ctx
    w = RW_WIDTH
    c = CHUNK
    nc, nl = ctx // c, seq // c

    def specs(rev):
        ck = lambda s: _stream_chunk(s, rev, nc, nl)
        return [pl.BlockSpec((nb, c, w), lambda s: (0, ck(s), 0)),
                pl.BlockSpec((nb, c, w), lambda s: (0, ck(s), 1)),
                pl.BlockSpec((nb, c, w), lambda s: (0, ck(s), 2)),
                pl.BlockSpec((nb, c, LANES), lambda s: (0, ck(s), Z_LORA // LANES))]

    consts = [p['k_k'], p['k_a'], p['r_k'], p['w0'], p['w_up'], p['a0'], p['a_up'], p['bones64']]
    out = jax.ShapeDtypeStruct((nb, lt, w), F32)
    fwd = pl.BlockSpec((nb, c, w), lambda s: (0, _stream_chunk(s, False, nc, nl), 0))
    return pl.pallas_call(
        functools.partial(_rwkv_kernel, nb=nb),
        grid=(nc + nl,),
        in_specs=specs(False) + specs(True) + [_const(a.shape) for a in consts],
        out_specs=[fwd, pl.BlockSpec((nb, c, w), lambda s: (0, _stream_chunk(s, True, nc, nl), 0)), fwd],
        out_shape=[out, out, out],
        scratch_shapes=[pltpu.VMEM((2, nb, c, w), F32)],
        compiler_params=_params("arbitrary"),
        name="rwkv",
    )(zc, zc, zc, z, zc, zc, zc, z, *consts)


def _mlstm_kernel(qf, kf, vf, gcf, gtf, qr, kr, vr, gcr, gtr, gbr_ref, gbc_ref, sel_ref, hrow_ref, hf_ref, hr_ref,
                  c_scr, n_scr, m_scr, *, nb):
    c = CHUNK
    nh, dh, pw = ML_HEADS, ML_HEAD_DIM, ML_PAD

    @pl.when(pl.program_id(0) == 0)
    def _():
        c_scr[...] = jnp.zeros_like(c_scr)
        n_scr[...] = jnp.zeros_like(n_scr)
        m_scr[...] = jnp.zeros_like(m_scr)

    lane1 = lax.broadcasted_iota(jnp.int32, (1, LANES), 1)
    ti = lax.broadcasted_iota(jnp.int32, (c, c), 0)
    tj = lax.broadcasted_iota(jnp.int32, (c, c), 1)
    eye = (ti == tj).astype(BF16)
    hrow = hrow_ref[...]

    chains = []
    for d, (q_ref, k_ref, v_ref, gc_ref, gt_ref, h_ref) in enumerate(
            ((qf, kf, vf, gcf, gtf, hf_ref), (qr, kr, vr, gcr, gtr, hr_ref))):
        rev = d == 1
        tri = ((tj >= ti) if rev else (tj <= ti)).astype(BF16)
        valid_t = (ti >= tj) if rev else (ti <= tj)
        ig_lanes = jnp.logical_and(lane1 >= d * 8, lane1 < d * 8 + nh).astype(F32)
        fg_lanes = jnp.logical_and(lane1 >= d * 8 + nh, lane1 < d * 8 + 2 * nh).astype(F32)
        for b in range(nb):
            q = q_ref[b]
            q = (q * jax.nn.sigmoid(q)).astype(BF16)
            k = k_ref[b]
            k = (k * jax.nn.sigmoid(k) * (dh ** -0.5)).astype(BF16)
            v = v_ref[b].astype(BF16)
            gc = gc_ref[b] + gbr_ref[...]
            gt = gt_ref[b, 0] + gbc_ref[...]
            bc_cols = _mm_xl(tri, -_softplus(-gc))
            bc_rows = _mm_xr(-_softplus(-gt), tri, _NT)
            cm = _mm_xr(gc * ig_lanes - bc_cols * fg_lanes, sel_ref[d])
            n_prev = n_scr[d, b]
            chains.append(dict(d=d, b=b, rev=rev, valid_t=valid_t, q=q, k=k, v=v, gt=gt, bc_rows=bc_rows, cm=cm,
                               qt=_dot(q, eye, _TN), kt=_dot(k, eye, _TN),
                               nq=_dot((hrow * n_prev).astype(BF16), q, _NT),
                               n_prev=n_prev, h_ref=h_ref, m_all=m_scr[d, b]))
    for ch in chains:
        d, rev = ch['d'], ch['rev']
        heads = []
        for h in range(nh):
            hs = slice(h * pw, (h + 1) * pw)
            ci, cf = d * 8 + h, d * 8 + nh + h
            bcr = ch['bc_rows'][cf:cf + 1, :]
            icr = ch['gt'][ci:ci + 1, :]
            b_last = bcr[:, 0:1] if rev else bcr[:, c - 1:c]
            m_prev = ch['m_all'][:, h:h + 1]
            lt = jnp.where(ch['valid_t'], bcr + ch['cm'][:, h * c:(h + 1) * c], -jnp.inf)
            log_inter = bcr + m_prev
            m = jnp.maximum(log_inter, jnp.max(lt, axis=0, keepdims=True))
            inter = jnp.exp(log_inter - m)
            sm_t = _dot(ch['k'][:, hs], ch['q'][:, hs], _NT) * jnp.exp(lt - m)
            den = inter * ch['nq'][h:h + 1, :] + jnp.sum(sm_t, axis=0, keepdims=True)
            w1 = 1.0 / jnp.maximum(jnp.abs(den), jnp.exp(-m))
            log_w = b_last - bcr + icr
            m_new = jnp.maximum(b_last + m_prev, jnp.max(log_w, axis=-1, keepdims=True))
            heads.append(dict(hs=hs, sm_w=(sm_t * w1).astype(BF16), qt_w=(ch['qt'][hs, :] * (inter * w1)).astype(BF16),
                              m_new=m_new, wgt=jnp.exp(log_w - m_new), dec=jnp.exp(b_last + m_prev - m_new)))
        ch['heads'] = heads
    for ch in chains:
        d, b = ch['d'], ch['b']
        for h, hd in enumerate(ch['heads']):
            hs = hd['hs']
            lhs_t = jnp.concatenate([hd['sm_w'], hd['qt_w']], axis=0)
            rhs = jnp.concatenate([ch['v'][:, hs], c_scr[d, b, h].astype(BF16)], axis=0)
            ch['h_ref'][b, :, hs] = _dot(lhs_t, rhs, _TN)
    for ch in chains:
        d, b = ch['d'], ch['b']
        m_new_all = jnp.zeros((1, LANES), F32)
        wrows = jnp.zeros((SUBLANES, c), F32)
        dec_e = jnp.zeros((1, ML_PW), F32)
        sub = lax.broadcasted_iota(jnp.int32, (SUBLANES, 1), 0)
        for h, hd in enumerate(ch['heads']):
            hs = hd['hs']
            kt_w = (ch['kt'][hs, :] * hd['wgt']).astype(BF16)
            c_scr[d, b, h] = hd['dec'] * c_scr[d, b, h] + _dot(kt_w, ch['v'][:, hs])
            m_new_all = m_new_all + hd['m_new'] * (lane1 == h).astype(F32)
            wrows = wrows + hd['wgt'] * (sub == h).astype(F32)
            dec_e = dec_e + hd['dec'] * hrow[h:h + 1, :]
        n_add = jnp.sum(_dot(wrows.astype(BF16), ch['k']) * hrow, axis=0, keepdims=True)
        n_scr[d, b] = dec_e * ch['n_prev'] + n_add
        m_scr[d, b] = m_new_all


def _mlstm(zc_ml, z, glt, p, dims):
    nb, seq, ctx = dims
    lt = seq + ctx
    c = CHUNK
    nc, nl = ctx // c, seq // c
    pw = ML_PW

    def specs(rev):
        ck = lambda s: _stream_chunk(s, rev, nc, nl)
        return [pl.BlockSpec((nb, c, pw), lambda s: (0, ck(s), 0)),
                pl.BlockSpec((nb, c, pw), lambda s: (0, ck(s), 1)),
                pl.BlockSpec((nb, c, pw), lambda s: (0, ck(s), Z_MLV // pw)),
                pl.BlockSpec((nb, c, LANES), lambda s: (0, ck(s), Z_MLG // LANES)),
                pl.BlockSpec((nb, 1, 16, c), lambda s: (0, ck(s), 0, 0))]

    out = jax.ShapeDtypeStruct((nb, lt, pw), F32)
    consts = [p['gb_row'], p['gb_col'], p['ml_sel'], p['ml_hrow']]
    return pl.pallas_call(
        functools.partial(_mlstm_kernel, nb=nb),
        grid=(nc + nl,),
        in_specs=specs(False) + specs(True) + [_const(a.shape) for a in consts],
        out_specs=[pl.BlockSpec((nb, c, pw), lambda s: (0, _stream_chunk(s, False, nc, nl), 0)),
                   pl.BlockSpec((nb, c, pw), lambda s: (0, _stream_chunk(s, True, nc, nl), 0))],
        out_shape=[out, out],
        scratch_shapes=[pltpu.VMEM((2, nb, ML_HEADS, ML_PAD, ML_PAD), F32), pltpu.VMEM((2, nb, 1, pw), F32),
                        pltpu.VMEM((2, nb, 1, LANES), F32)],
        compiler_params=_params("arbitrary"),
        name="mlstm",
    )(zc_ml, zc_ml, z, z, glt, zc_ml, zc_ml, z, z, glt, *consts)


def _s5_kernel(u_ref, kt_ref, wb_ref, wc_ref, a_ref, y_ref, lhs_scr, st_scr, *, n, n_ctx):
    cs = S5_CHUNK
    hs = S5_HSTATE
    for i in range(cs):
        lhs_scr[:, i * S5_HALF:(i + 1) * S5_HALF] = u_ref[0, pl.ds(i, n, stride=cs), :].astype(BF16)
    lhs = lhs_scr[...]
    yacc = _dot(lhs, kt_ref[0])
    for d in range(2):
        st_scr[...] = _dot(lhs, wb_ref[d, 0])
        a = a_ref[d, 0]
        ar, ai = a[:, :hs], a[:, hs:]

        def body(t, x, d=d, ar=ar, ai=ai):
            if d == 1:
                idx = jnp.where(t < n_ctx, n_ctx - 1 - t, n + n_ctx - 1 - t)
            else:
                idx = t
            xr, xi = x
            row = st_scr[pl.ds(idx, 1), :]
            st_scr[pl.ds(idx, 1), :] = jnp.concatenate([xr, xi], axis=1)
            return (ar * xr - ai * xi + row[:, :hs], ar * xi + ai * xr + row[:, hs:])

        zero = jnp.zeros((1, hs), F32)
        lax.fori_loop(0, n, body, (zero, zero))
        yacc = yacc + _dot(st_scr[...].astype(BF16), wc_ref[d, 0])
    for j in range(cs):
        y_ref[0, pl.ds(j, n, stride=cs), :] = yacc[:, j * S5_HALF:(j + 1) * S5_HALF]


def _s5_branch(z, sp, dims):
    nb, seq, ctx = dims
    lt = seq + ctx
    n = lt // S5_CHUNK
    kw = S5_CHUNK * S5_HALF
    hs2 = 2 * S5_HSTATE
    kern = functools.partial(_s5_kernel, n=n, n_ctx=ctx // S5_CHUNK)
    once = pl.Buffered(1)

    def half(g):
        return pl.pallas_call(
            kern,
            grid=(nb,),
            in_specs=[pl.BlockSpec((1, lt, S5_HALF), lambda b: (b, 0, Z_S5 // S5_HALF + g)),
                      pl.BlockSpec((1, kw, kw), lambda b: (g, 0, 0), pipeline_mode=once),
                      pl.BlockSpec((2, 1, kw, hs2), lambda b: (0, g, 0, 0), pipeline_mode=once),
                      pl.BlockSpec((2, 1, hs2, kw), lambda b: (0, g, 0, 0), pipeline_mode=once),
                      pl.BlockSpec((2, 1, 1, hs2), lambda b: (0, g, 0, 0), pipeline_mode=once)],
            out_specs=pl.BlockSpec((1, lt, S5_HALF), lambda b: (b, 0, 0)),
            out_shape=jax.ShapeDtypeStruct((nb, lt, S5_HALF), F32),
            scratch_shapes=[pltpu.VMEM((n, kw), BF16), pltpu.VMEM((n, hs2), F32)],
            compiler_params=_params("parallel"),
            name="s5_scan",
        )(z, sp['ktoep'], sp['wb'], sp['wc'], sp['a_chunk'])

    return half(0), half(1)


def _s5_operators(p):
    cs = S5_CHUNK
    hch, ns = S5_GROUP_CH, S5_STATE
    gh = 8

    def block_diag(slab, width):
        rows = slab.shape[-2]
        r = rows // gh
        tiled = jnp.concatenate([slab] * gh, axis=-1)
        rg = (np.arange(rows) // r)[:, None]
        cg = (np.arange(gh * width) // width)[None, :]
        return tiled * jnp.asarray((rg == cg).astype(np.float32))

    kts, wbs, wcs, acs = [0.0, 0.0], [], [], []
    for d in range(2):
        lr = jnp.minimum(p['s5_a_re'][d].astype(F32), -1e-4)
        li = p['s5_a_im'][d].astype(F32)
        dt = jnp.exp(p['s5_log_dt'][d].astype(F32))[:, None]
        tt = jnp.arange(cs + 1, dtype=F32)[:, None, None]
        mag = jnp.exp(lr * dt * tt)
        pr, pi = mag * jnp.cos(li * dt * tt), mag * jnp.sin(li * dt * tt)
        den = lr * lr + li * li
        qr = ((pr[1] - 1.0) * lr + pi[1] * li) / den
        qi = (pi[1] * lr - (pr[1] - 1.0) * li) / den
        b_re, b_im = p['s5_b_re'][d].astype(F32), p['s5_b_im'][d].astype(F32)
        bbr = qr[..., None] * b_re - qi[..., None] * b_im
        bbi = qr[..., None] * b_im + qi[..., None] * b_re
        cr, ci = p['s5_c_re'][d].astype(F32), p['s5_c_im'][d].astype(F32)
        abr = pr[:cs, :, :, None] * bbr[None] - pi[:cs, :, :, None] * bbi[None]
        abi = pr[:cs, :, :, None] * bbi[None] + pi[:cs, :, :, None] * bbr[None]
        ktau = jnp.einsum('ghn,tgni->tgih', cr, abr) - jnp.einsum('ghn,tgni->tgih', ci, abi)
        ii = np.arange(cs)[:, None]
        jj = np.arange(cs)[None, :]
        lag = (ii - jj) if d == 1 else (jj - ii)
        live = jnp.asarray((lag >= 0).astype(np.float32))[:, :, None, None]
        e_in = np.arange(cs) if d == 1 else (cs - 1 - np.arange(cs))
        f_out = (cs - np.arange(cs)) if d == 1 else (np.arange(cs) + 1)
        wcr = cr[None] * pr[f_out][:, :, None, :] - ci[None] * pi[f_out][:, :, None, :]
        wci = cr[None] * pi[f_out][:, :, None, :] + ci[None] * pr[f_out][:, :, None, :]
        wb_h, wc_h, ac_h = [], [], []
        for half in range(2):
            gs = slice(half * gh, (half + 1) * gh)
            bd_tau = block_diag(ktau[:, gs].reshape(cs, S5_HALF, hch), hch)
            kts[half] = kts[half] + bd_tau[np.clip(lag, 0, cs - 1)] * live
            wre = block_diag(jnp.swapaxes(abr[e_in][:, gs], 2, 3).reshape(cs, S5_HALF, ns), ns)
            wim = block_diag(jnp.swapaxes(abi[e_in][:, gs], 2, 3).reshape(cs, S5_HALF, ns), ns)
            wb_h.append(jnp.concatenate([wre, wim], axis=2).reshape(cs * S5_HALF, 2 * S5_HSTATE))
            cre = block_diag(jnp.swapaxes(wcr[:, gs], 2, 3).reshape(cs, S5_HSTATE, hch), hch)
            cim = block_diag(jnp.swapaxes(-wci[:, gs], 2, 3).reshape(cs, S5_HSTATE, hch), hch)
            wc_h.append(jnp.concatenate([jnp.swapaxes(cre, 0, 1).reshape(S5_HSTATE, cs * S5_HALF),
                                         jnp.swapaxes(cim, 0, 1).reshape(S5_HSTATE, cs * S5_HALF)], axis=0))
            ac_h.append(jnp.concatenate([pr[cs][gs].reshape(1, S5_HSTATE), pi[cs][gs].reshape(1, S5_HSTATE)], axis=1))
        wbs.append(jnp.stack(wb_h))
        wcs.append(jnp.stack(wc_h))
        acs.append(jnp.stack(ac_h))
    dsk = p['s5_d'].astype(F32).reshape(2, S5_HALF)
    eye_c = jnp.asarray(np.eye(cs, dtype=np.float32))[:, :, None, None]
    eye_l = jnp.asarray(np.eye(S5_HALF, dtype=np.float32))
    kt_h = []
    for half in range(2):
        kt = kts[half] + eye_c * (eye_l * dsk[half][None, :])[None, None]
        kt_h.append(jnp.swapaxes(kt, 1, 2).reshape(cs * S5_HALF, cs * S5_HALF))
    return dict(ktoep=jnp.stack(kt_h).astype(BF16), wb=jnp.stack(wbs).astype(BF16),
                wc=jnp.stack(wcs).astype(BF16), a_chunk=jnp.stack(acs))


def _merge_kernel(x_ref, m_ref, yf_ref, yr_ref, bonus_ref, gdn_ref, hf_ref, hr_ref, o_ref, s5a_ref, s5b_ref,
                  br0_ref, br1_ref, br2_ref,
                  gup_ref, gng_ref, gnb_ref, bones64_ref, glw_ref, glb_ref, mlg_ref, bones_ml_ref, mlmask_ref,
                  brb_ref, uprw_ref, ups5_ref, upml_ref, wout_ref, lng_ref, lnb_ref, out_ref, *, nb, tr, ctx_tiles):
    d = D_MODEL
    is_ctx = pl.program_id(0) < ctx_tiles
    rows = nb * tr
    flat = lambda ref: ref[...].reshape(rows, ref.shape[-1])
    ys = flat(yf_ref) + flat(yr_ref)
    b64 = bones64_ref[...]
    yc = ys - _mm_xr(ys, b64) * (1.0 / RW_HEAD_DIM)
    var = _mm_xr(yc * yc, b64) * (1.0 / RW_HEAD_DIM)
    yn = yc * lax.rsqrt(var + RW_GN_EPS) * gng_ref[...] + gnb_ref[...]
    rw_y = (yn + flat(bonus_ref)) * _mm(jax.nn.sigmoid(flat(gdn_ref)), gup_ref[...])
    s5 = jax.nn.gelu(jnp.concatenate([flat(s5a_ref), flat(s5b_ref)], axis=1))
    s5_y = s5 * jax.nn.sigmoid(_mm(s5, glw_ref[...]) + glb_ref[...])
    hg = jax.nn.sigmoid(flat(o_ref)) * (flat(hf_ref) + flat(hr_ref))
    bml = bones_ml_ref[...]
    hc = (hg - _mm_xr(hg, bml) * (1.0 / ML_HEAD_DIM)) * mlmask_ref[...]
    hv = _mm_xr(hc * hc, bml) * (1.0 / ML_HEAD_DIM)
    ml_y = hc * lax.rsqrt(hv + ML_EPS) * mlg_ref[...]
    brb = brb_ref[...]
    y = (jax.nn.sigmoid(flat(br0_ref) + brb[:, 0:d]) * _mm(rw_y, uprw_ref[...])
         + jax.nn.sigmoid(flat(br1_ref) + brb[:, d:2 * d]) * _mm(s5_y, ups5_ref[...])
         + jax.nn.sigmoid(flat(br2_ref) + brb[:, 2 * d:3 * d]) * _mm(ml_y, upml_ref[...]))
    out = _mm(y, wout_ref[...])
    for b in range(nb):
        gate = _mod_vec(m_ref, is_ctx, b, nb, 5)
        out_ref[b] = _layer_norm(DEEPNORM_ALPHA * x_ref[b] + gate * out[b * tr:(b + 1) * tr, :],
                                 lng_ref[...], lnb_ref[...])


def _merge(xs, mods, yf, yr, bonus, hf, hr, s5y, z, p, dims):
    nb, seq, ctx = dims
    lt = seq + ctx
    d = D_MODEL
    tr = _pick_tile((128, 64), seq, ctx)
    row = lambda width, col=0: pl.BlockSpec((nb, tr, width), lambda i: (0, i, col))
    consts = [p['g_up'], p['gn_g'], p['gn_b'], p['bones64'], p['glu_w'], p['glu_b'], p['ml_norm_g'], p['bones_ml'],
              p['ml_mask'], p['br_b'], p['up_rw'], p['up_s5'], p['up_ml'], p['w_out'], p['ln_g1'], p['ln_b1']]
    kern = functools.partial(_merge_kernel, nb=nb, tr=tr, ctx_tiles=ctx // tr)
    return pl.pallas_call(
        kern,
        grid=(lt // tr,),
        in_specs=[row(d), _const(mods.shape),
                  row(RW_WIDTH), row(RW_WIDTH), row(RW_WIDTH), row(RW_GATE_LORA, Z_GDN // RW_GATE_LORA),
                  row(ML_PW), row(ML_PW), row(ML_PW, Z_MLO // ML_PW), row(S5_HALF), row(S5_HALF),
                  row(d, Z_BR // d), row(d, Z_BR // d + 1), row(d, Z_BR // d + 2)] + [_const(a.shape) for a in consts],
        out_specs=row(d),
        out_shape=jax.ShapeDtypeStruct((nb, lt, d), F32),
        compiler_params=_params("parallel"),
        name="merge",
    )(xs, mods, yf, yr, bonus, z, hf, hr, z, s5y[0], s5y[1], z, z, z, *consts)


def _pad_heads(w, axis):
    shape = w.shape
    w = w.reshape(shape[:axis] + (ML_HEADS, ML_HEAD_DIM) + shape[axis + 1:])
    pad = [(0, 0)] * w.ndim
    pad[axis + 1] = (0, ML_PAD - ML_HEAD_DIM)
    return jnp.pad(w, pad).reshape(shape[:axis] + (ML_PW,) + shape[axis + 1:])


def _permute_w_in(w):
    old = {}
    start = 0
    for name, size in (('rw_conv', 3 * RW_WIDTH), ('ml_q', ML_WIDTH), ('ml_k', ML_WIDTH), ('ml_v', ML_WIDTH),
                       ('ml_o', ML_WIDTH), ('ml_gl', 4 * ML_HEADS), ('s5_u', S5_WIDTH), ('w_dn', RW_DECAY_LORA),
                       ('a_dn', RW_AAA_LORA), ('g_dn', RW_GATE_LORA), ('br_gl', N_BRANCH * D_MODEL)):
        old[name] = w[:, start:start + size]
        start += size
    zeros = lambda n: jnp.zeros((w.shape[0], n), w.dtype)
    out = jnp.concatenate([old['rw_conv'], old['w_dn'], old['a_dn'], old['g_dn'], old['ml_gl'],
                           zeros(LANES - 4 * ML_HEADS), old['s5_u'], zeros(Z_MLV - Z_S5 - S5_WIDTH),
                           _pad_heads(old['ml_v'], 1), _pad_heads(old['ml_o'], 1), _pad_heads(old['ml_q'], 1),
                           _pad_heads(old['ml_k'], 1), old['br_gl']], axis=1)
    assert out.shape[1] == Z_WIDTH
    return out


def _ml_gate_select():
    sel = np.zeros((2, LANES, ML_HEADS * CHUNK), np.float32)
    for d in range(2):
        for h in range(ML_HEADS):
            sel[d, d * 8 + h, h * CHUNK:(h + 1) * CHUNK] = 1.0
            sel[d, d * 8 + ML_HEADS + h, h * CHUNK:(h + 1) * CHUNK] = 1.0
    return sel


def _ml_head_rows():
    rows = np.zeros((SUBLANES, ML_PW), np.float32)
    for h in range(ML_HEADS):
        rows[h, h * ML_PAD:(h + 1) * ML_PAD] = 1.0
    return rows


def _block_ones(width, group):
    g = np.arange(width) // group
    return (g[:, None] == g[None, :]).astype(np.float32)


def _layer_params(i, a):
    w = RW_WIDTH
    pad_lo = jnp.zeros((2, RW_AAA_LORA, w), F32)
    gb = a['ml_gate_b'][i].reshape(16).astype(F32)
    ml_real = (np.arange(ML_PW) % ML_PAD < ML_HEAD_DIM).astype(np.float32)
    return dict(
        k_k=a['rw_k_k'][i].reshape(1, w), k_a=a['rw_k_a'][i].reshape(1, w), r_k=a['rw_r_k'][i].reshape(1, w),
        w0=a['rw_w0'][i].reshape(2, 1, w), a0=a['rw_a0'][i].reshape(2, 1, w),
        w_up=jnp.concatenate([a['rw_w_up'][i], pad_lo], axis=1).astype(BF16),
        a_up=jnp.concatenate([pad_lo, a['rw_a_up'][i]], axis=1).astype(BF16),
        bones64=jnp.asarray(_block_ones(w, RW_HEAD_DIM), BF16),
        bones_ml=jnp.asarray(_block_ones(ML_PW, ML_PAD), BF16),
        ml_mask=jnp.asarray(ml_real.reshape(1, ML_PW)),
        ml_sel=jnp.asarray(_ml_gate_select(), BF16), ml_hrow=jnp.asarray(_ml_head_rows()),
        gb_row=jnp.concatenate([gb, jnp.zeros((LANES - 16,), F32)]).reshape(1, LANES), gb_col=gb.reshape(16, 1),
        g_up=a['rw_g_up'][i].astype(BF16), gn_g=a['rw_gn_g'][i].reshape(1, w), gn_b=a['rw_gn_b'][i].reshape(1, w),
        glu_w=a['s5_glu_w'][i].astype(BF16), glu_b=a['s5_glu_b'][i].reshape(1, S5_WIDTH),
        ml_norm_g=_pad_heads(a['ml_norm_g'][i].reshape(1, ML_WIDTH), 1),
        br_b=a['br_gate_b'][i].reshape(1, N_BRANCH * D_MODEL),
        up_rw=a['up_rw'][i].astype(BF16), up_s5=a['up_s5'][i].astype(BF16),
        up_ml=_pad_heads(a['up_ml'][i], 0).astype(BF16),
        w_out=a['w_out'][i].astype(BF16),
        ln_g1=a['ln_g'][i, 1].reshape(1, D_MODEL), ln_b1=a['ln_b'][i, 1].reshape(1, D_MODEL),
        conv_rw=a['conv_w'][i][:, :, :3 * RW_WIDTH],
        conv_ml=jnp.concatenate([_pad_heads(a['conv_w'][i][:, :, 3 * RW_WIDTH:3 * RW_WIDTH + ML_WIDTH], 2),
                                 _pad_heads(a['conv_w'][i][:, :, 3 * RW_WIDTH + ML_WIDTH:], 2)], axis=2),
    )


def kernel(x, c, ctx, c_ctx, ada_w, ada_b, ln_g, ln_b, ffn_w_gate, ffn_w_up, ffn_w_down, w_in, conv_w, rw_w0, rw_w_up, rw_a0, rw_a_up, rw_g_up, rw_k_k, rw_k_a, rw_r_k, rw_gn_g, rw_gn_b, s5_a_re, s5_a_im, s5_log_dt, s5_b_re, s5_b_im, s5_c_re, s5_c_im, s5_d, s5_glu_w, s5_glu_b, ml_gate_b, ml_norm_g, up_rw, up_s5, up_ml, br_gate_b, w_out):
    a = dict(ln_g=ln_g, ln_b=ln_b, conv_w=conv_w, rw_w0=rw_w0, rw_w_up=rw_w_up, rw_a0=rw_a0, rw_a_up=rw_a_up,
             rw_g_up=rw_g_up, rw_k_k=rw_k_k, rw_k_a=rw_k_a, rw_r_k=rw_r_k, rw_gn_g=rw_gn_g, rw_gn_b=rw_gn_b,
             s5_glu_w=s5_glu_w, s5_glu_b=s5_glu_b, ml_gate_b=ml_gate_b, ml_norm_g=ml_norm_g, up_rw=up_rw,
             up_s5=up_s5, up_ml=up_ml, br_gate_b=br_gate_b, w_out=w_out)
    nb, seq, d = x.shape
    nctx = ctx.shape[1]
    dims = (nb, seq, nctx)
    lt = seq + nctx
    depth = ada_w.shape[0]
    rows = -(-(nb + 1) // SUBLANES) * SUBLANES
    cv = jnp.concatenate([c, c_ctx[None, :], jnp.zeros((rows - nb - 1, d), F32)], axis=0)
    mods = _modulation(cv, ada_w, ada_b)
    ffn_bf16 = (ffn_w_gate.astype(BF16), ffn_w_up.astype(BF16), ffn_w_down.astype(BF16))
    xs = None
    for i in range(depth):
        m = mods[i]
        p = _layer_params(i, a)
        sp = _s5_operators(dict(s5_a_re=s5_a_re[i], s5_a_im=s5_a_im[i], s5_log_dt=s5_log_dt[i], s5_b_re=s5_b_re[i],
                                s5_b_im=s5_b_im[i], s5_c_re=s5_c_re[i], s5_c_im=s5_c_im[i], s5_d=s5_d[i]))
        w_in_p = _permute_w_in(w_in[i].astype(BF16))
        ffn_w = lambda j: (*ffn_bf16, (i, j))
        tr = _ffn_tile(dims)
        ct, st = nctx // tr, seq // tr
        ln0 = (ln_g[i, 0], ln_b[i, 0])
        if i == 0:
            xs = _ffn_half(x, m, 0, *ffn_w(0), *ln0, dims, src_tile0=0, n_tiles=ct + st, n_ctx_tiles=ct,
                           dst_rows=lt, ctx_src=ctx)
        else:
            xs = _ffn_half(xs, m, 0, *ffn_w(0), *ln0, dims, src_tile0=0, n_tiles=ct + st, n_ctx_tiles=ct,
                           dst_rows=lt)
        gl0 = 3 * RW_WIDTH + 4 * ML_WIDTH
        w_gate_t = jnp.pad(w_in[i][:, gl0:gl0 + 4 * ML_HEADS].T.astype(BF16), ((0, LANES - 4 * ML_HEADS), (0, 0)))
        z, glt = _in_proj(xs, m, w_in_p, w_gate_t, dims)
        zc_rw = _short_conv(z, p['conv_rw'], Z_RWC, dims)
        zc_ml = _short_conv(z, p['conv_ml'], Z_MLC, dims)
        yf, yr, bonus = _rwkv(zc_rw, z, p, dims)
        hf, hr = _mlstm(zc_ml, z, glt, p, dims)
        s5y = _s5_branch(z, sp, dims)
        xs = _merge(xs, m, yf, yr, bonus, hf, hr, s5y, z, p, dims)
        if i == depth - 1:
            xs = _ffn_half(xs, m, 6, *ffn_w(1), ln_g[i, 2], ln_b[i, 2], dims, src_tile0=ct, n_tiles=st,
                           n_ctx_tiles=0, dst_rows=seq)
        else:
            xs = _ffn_half(xs, m, 6, *ffn_w(1), ln_g[i, 2], ln_b[i, 2], dims, src_tile0=0, n_tiles=ct + st,
                           n_ctx_tiles=ct, dst_rows=lt)
    return xs
```
